```python
import math
import jax, jax.numpy as jnp
from jax import lax
import numpy as np

D_MODEL = 2048
BATCH = 2
SEQ = 8192
DEPTH = 1
DEC_BATCH = 32
DEC_SEQ = 1
PAST_LEN = 16384
PAGE_SIZE = 128

MIX_WIDTH = D_MODEL
DA_WIDTH = MIX_WIDTH // 2
GLA_WIDTH = MIX_WIDTH - DA_WIDTH
DA_HEADS = 8
DA_VDIM = DA_WIDTH // DA_HEADS
DA_QKDIM = DA_VDIM // 2
GLA_HEADS = 4
GLA_VDIM = GLA_WIDTH // GLA_HEADS
GLA_KDIM = GLA_VDIM // 2
GLA_KWIDTH = GLA_HEADS * GLA_KDIM
GATE_RANK = 16
GATE_NORMALIZER = 16.0
GLA_CHUNK = 64
Q_BLOCK = 128
D_FF = 4 * D_MODEL
EPS = 1e-6
N_IN = 3 * DA_WIDTH + 2 * GLA_KWIDTH + GLA_WIDTH + GATE_RANK + GLA_WIDTH

kernel_name = "hymba_diffattn_gla_decoder_step"


def _rms(x, g):
    xf = x.astype(jnp.float32)
    y = xf * lax.rsqrt(jnp.mean(xf * xf, axis=-1, keepdims=True) + EPS)
    return (y * g.astype(jnp.float32)).astype(x.dtype)


def _adaln(c, ada_w, ada_b):
    m = jnp.einsum("bd,dn->bn", jax.nn.silu(c), ada_w) + ada_b
    return jnp.split(m[:, None, :], 6, axis=-1)


def _lambda(lq1, lk1, lq2, lk2, lam_init):
    f = jnp.float32
    return (jnp.exp(jnp.sum(lq1.astype(f) * lk1.astype(f)))
            - jnp.exp(jnp.sum(lq2.astype(f) * lk2.astype(f))) + lam_init)


def _mixer_inputs(x, shift, scale, norm_g, w_in, q_norm_g, k_norm_g, gate_w2, gate_b):
    B, T, _ = x.shape
    h = _rms(x, norm_g) * (1 + scale) + shift
    p = jnp.einsum("btd,dn->btn", h, w_in)
    cuts = [int(c) for c in np.cumsum([DA_WIDTH, DA_WIDTH, DA_WIDTH, GLA_KWIDTH, GLA_KWIDTH, GLA_WIDTH, GATE_RANK])]
    q, k, v, gq, gk, gv, glr, gout = jnp.split(p, cuts, axis=-1)
    q = _rms(q.reshape(B, T, DA_HEADS, 2, DA_QKDIM), q_norm_g)
    k = _rms(k.reshape(B, T, DA_HEADS, 2, DA_QKDIM), k_norm_g)
    v = v.reshape(B, T, DA_HEADS, DA_VDIM)
    gq = gq.reshape(B, T, GLA_HEADS, GLA_KDIM).astype(jnp.float32) * (GLA_KDIM ** -0.5)
    gk = gk.reshape(B, T, GLA_HEADS, GLA_KDIM).astype(jnp.float32)
    gv = gv.reshape(B, T, GLA_HEADS, GLA_VDIM).astype(jnp.float32)
    glog = jax.nn.log_sigmoid(
        jnp.einsum("btr,rn->btn", glr, gate_w2, preferred_element_type=jnp.float32)
        + gate_b.astype(jnp.float32)) / GATE_NORMALIZER
    glog = glog.reshape(B, T, GLA_HEADS, GLA_KDIM)
    return q, k, v, gq, gk, gv, glog, gout


def _diff_attn_prompt(q, k, v, lam):
    B, T = q.shape[:2]
    nb = T // Q_BLOCK
    sc = DA_QKDIM ** -0.5
    qb = q.reshape(B, nb, Q_BLOCK, DA_HEADS, 2, DA_QKDIM).transpose(1, 0, 2, 3, 4, 5)
    kpos = jnp.arange(T)

    def block(args):
        qi, i = args
        s = jnp.einsum("bqhcd,bkhcd->cbhqk", qi, k, preferred_element_type=jnp.float32) * sc
        qpos = i * Q_BLOCK + jnp.arange(Q_BLOCK)
        s = jnp.where(kpos[None, :] <= qpos[:, None], s, -jnp.inf)
        a = jax.nn.softmax(s, axis=-1)
        w = a[0] - lam * a[1]
        return jnp.einsum("bhqk,bkhv->bqhv", w, v, preferred_element_type=jnp.float32)

    o = lax.map(block, (qb, jnp.arange(nb)))
    return o.transpose(1, 0, 2, 3, 4).reshape(B, T, DA_HEADS, DA_VDIM)


def _diff_attn_sample(q, k, v, cache_k, cache_v, page_table, layer, lam):
    DB, T = q.shape[:2]
    sc = DA_QKDIM ** -0.5
    kp = cache_k[layer, page_table]
    vp = cache_v[layer, page_table]
    P = kp.shape[1] * kp.shape[2]
    kp = kp.reshape(DB, P, DA_HEADS, 2, DA_QKDIM)
    vp = vp.reshape(DB, P, DA_HEADS, DA_VDIM)
    s_past = jnp.einsum("bqhcd,bkhcd->cbhqk", q, kp, preferred_element_type=jnp.float32) * sc
    s_new = jnp.einsum("bqhcd,bkhcd->cbhqk", q, k, preferred_element_type=jnp.float32) * sc
    causal = jnp.tril(jnp.ones((T, T), dtype=bool))
    s_new = jnp.where(causal, s_new, -jnp.inf)
    a = jax.nn.softmax(jnp.concatenate([s_past, s_new], axis=-1), axis=-1)
    w = a[0] - lam * a[1]
    o = (jnp.einsum("bhqk,bkhv->bqhv", w[..., :P], vp, preferred_element_type=jnp.float32)
         + jnp.einsum("bhqk,bkhv->bqhv", w[..., P:], v, preferred_element_type=jnp.float32))
    return o


def _gla_prompt(q, k, v, g):
    B, T, H, dk = q.shape
    dv = v.shape[-1]
    C = GLA_CHUNK
    n = T // C
    q = q.reshape(B, n, C, H, dk)
    k = k.reshape(B, n, C, H, dk)
    v = v.reshape(B, n, C, H, dv)
    bc = jnp.cumsum(g.reshape(B, n, C, H, dk), axis=2)
    b_last = bc[:, :, -1:]
    q_t = q * jnp.exp(bc)
    k_t = k * jnp.exp(-bc)
    k_e = k * jnp.exp(b_last - bc)
    att = jnp.einsum("bnchd,bnshd->bnhcs", q_t, k_t)
    att = jnp.where(jnp.tril(jnp.ones((C, C), dtype=bool)), att, 0.0)
    o_intra = jnp.einsum("bnhcs,bnshv->bnchv", att, v)
    kv = jnp.einsum("bnchd,bnchv->bnhdv", k_e, v)
    decay = jnp.exp(b_last[:, :, 0])

    def step(S, inp):
        kv_c, dec_c = inp
        return dec_c[..., None] * S + kv_c, S

    S0 = jnp.zeros((B, H, dk, dv), jnp.float32)
    S_fin, S_prev = lax.scan(step, S0, (kv.transpose(1, 0, 2, 3, 4), decay.transpose(1, 0, 2, 3)))
    o_inter = jnp.einsum("bnchd,nbhdv->bnchv", q_t, S_prev)
    return (o_intra + o_inter).reshape(B, T, H, dv), S_fin


def _gla_sample(q, k, v, g, S0):
    def step(S, inp):
        qt, kt, vt, gt = inp
        S = jnp.exp(gt)[..., None] * S + kt[..., :, None] * vt[..., None, :]
        return S, jnp.einsum("bhd,bhdv->bhv", qt, S)

    tr = lambda a: jnp.swapaxes(a, 0, 1)
    S, o = lax.scan(step, S0.astype(jnp.float32), (tr(q), tr(k), tr(v), tr(g)))
    return tr(o), S


def _finish(x, o_da, o_gla, gout, gate1, shift2, scale2, gate2, lam_init,
            diff_subln_g, gla_norm_g, w_out, norm2_g, w_up, w_down):
    B, T, _ = x.shape
    o_da = (_rms(o_da, diff_subln_g) * (1.0 - lam_init)).reshape(B, T, DA_WIDTH)
    o_gla = (_rms(o_gla, gla_norm_g)
             * jax.nn.silu(gout.astype(jnp.float32)).reshape(B, T, GLA_HEADS, GLA_VDIM)).reshape(B, T, GLA_WIDTH)
    mix = jnp.concatenate([o_da, o_gla], axis=-1).astype(x.dtype)
    x = x + gate1 * jnp.einsum("btm,md->btd", mix, w_out)
    h = _rms(x, norm2_g) * (1 + scale2) + shift2
    u = jnp.square(jax.nn.relu(jnp.einsum("btd,df->btf", h, w_up)))
    return x + gate2 * jnp.einsum("btf,fd->btd", u, w_down)


def setup_inputs(seed: int = 0) -> dict:
    key = jax.random.key(seed)
    ks = jax.random.split(key, 32)
    f = jnp.float32
    n_pages = PAST_LEN // PAGE_SIZE
    n_used = DEC_BATCH * n_pages
    n_pool = (n_used * 5) // 4
    nrm = lambda k, s, sc: jax.random.normal(k, s, f) * sc
    perm = jax.random.permutation(ks[0], n_pool)[:n_used]
    return {
        "x_prompt": nrm(ks[1], (BATCH, SEQ, D_MODEL), 1.0),
        "x_sample": nrm(ks[2], (DEC_BATCH, DEC_SEQ, D_MODEL), 1.0),
        "cache_k": nrm(ks[3], (DEPTH, n_pool, PAGE_SIZE, DA_HEADS, 2 * DA_QKDIM), 1.0),
        "cache_v": nrm(ks[4], (DEPTH, n_pool, PAGE_SIZE, DA_HEADS, DA_VDIM), 1.0),
        "state_gla": nrm(ks[5], (DEPTH, DEC_BATCH, GLA_HEADS, GLA_KDIM, GLA_VDIM), 1.0),
        "page_table": perm.reshape(DEC_BATCH, n_pages).astype(jnp.int32),
        "c_prompt": nrm(ks[6], (BATCH, D_MODEL), 1.0),
        "c_sample": nrm(ks[7], (DEC_BATCH, D_MODEL), 1.0),
        "norm1_g": 1.0 + nrm(ks[8], (DEPTH, D_MODEL), 0.02),
        "ada_w": nrm(ks[9], (DEPTH, D_MODEL, 6 * D_MODEL), D_MODEL ** -0.5),
        "ada_b": nrm(ks[10], (DEPTH, 6 * D_MODEL), 0.02),
        "w_in": nrm(ks[11], (DEPTH, D_MODEL, N_IN), D_MODEL ** -0.5),
        "q_norm_g": 1.0 + nrm(ks[12], (DEPTH, DA_QKDIM), 0.02),
        "k_norm_g": 1.0 + nrm(ks[13], (DEPTH, DA_QKDIM), 0.02),
        "lambda_q1": nrm(ks[14], (DEPTH, DA_QKDIM), 0.1),
        "lambda_k1": nrm(ks[15], (DEPTH, DA_QKDIM), 0.1),
        "lambda_q2": nrm(ks[16], (DEPTH, DA_QKDIM), 0.1),
        "lambda_k2": nrm(ks[17], (DEPTH, DA_QKDIM), 0.1),
        "diff_subln_g": 1.0 + nrm(ks[18], (DEPTH, DA_VDIM), 0.02),
        "gla_gate_w2": nrm(ks[19], (DEPTH, GATE_RANK, GLA_KWIDTH), GATE_RANK ** -0.5),
        "gla_gate_b": nrm(ks[20], (DEPTH, GLA_KWIDTH), 0.02),
        "gla_norm_g": 1.0 + nrm(ks[21], (DEPTH, GLA_VDIM), 0.02),
        "w_out": nrm(ks[22], (DEPTH, MIX_WIDTH, D_MODEL), MIX_WIDTH ** -0.5),
        "norm2_g": 1.0 + nrm(ks[23], (DEPTH, D_MODEL), 0.02),
        "w_up": nrm(ks[24], (DEPTH, D_MODEL, D_FF), D_MODEL ** -0.5),
        "w_down": nrm(ks[25], (DEPTH, D_FF, D_MODEL), D_FF ** -0.5),
    }


def reference(x_prompt, x_sample, cache_k, cache_v, state_gla, page_table, c_prompt, c_sample,
              norm1_g, ada_w, ada_b, w_in, q_norm_g, k_norm_g, lambda_q1, lambda_k1, lambda_q2,
              lambda_k2, diff_subln_g, gla_gate_w2, gla_gate_b, gla_norm_g, w_out, norm2_g,
              w_up, w_down):
    xp, xs = x_prompt, x_sample
    nkp, nvp, nsp, nks, nvs, nss = [], [], [], [], [], []
    for i in range(DEPTH):
        lam_init = 0.8 - 0.6 * math.exp(-0.3 * i)
        lam = _lambda(lambda_q1[i], lambda_k1[i], lambda_q2[i], lambda_k2[i], lam_init)
        fin_w = (lam_init, diff_subln_g[i], gla_norm_g[i], w_out[i], norm2_g[i], w_up[i], w_down[i])
        in_w = (norm1_g[i], w_in[i], q_norm_g[i], k_norm_g[i], gla_gate_w2[i], gla_gate_b[i])

        sh1, sc1, g1, sh2, sc2, g2 = _adaln(c_prompt, ada_w[i], ada_b[i])
        q, k, v, gq, gk, gv, gl, gout = _mixer_inputs(xp, sh1, sc1, *in_w)
        o_da = _diff_attn_prompt(q, k, v, lam)
        o_gla, S_p = _gla_prompt(gq, gk, gv, gl)
        B, T = xp.shape[:2]
        nkp.append(k.reshape(B, T, DA_HEADS, 2 * DA_QKDIM))
        nvp.append(v)
        nsp.append(S_p.astype(state_gla.dtype))
        xp = _finish(xp, o_da, o_gla, gout, g1, sh2, sc2, g2, *fin_w)

        sh1, sc1, g1, sh2, sc2, g2 = _adaln(c_sample, ada_w[i], ada_b[i])
        q, k, v, gq, gk, gv, gl, gout = _mixer_inputs(xs, sh1, sc1, *in_w)
        o_da = _diff_attn_sample(q, k, v, cache_k, cache_v, page_table, i, lam)
        o_gla, S_s = _gla_sample(gq, gk, gv, gl, state_gla[i])
        DB, Ts = xs.shape[:2]
        nks.append(k.reshape(DB, Ts, DA_HEADS, 2 * DA_QKDIM))
        nvs.append(v)
        nss.append(S_s.astype(state_gla.dtype))
        xs = _finish(xs, o_da, o_gla, gout, g1, sh2, sc2, g2, *fin_w)

    return (xp, xs, jnp.stack(nkp), jnp.stack(nvp), jnp.stack(nsp),
            jnp.stack(nks), jnp.stack(nvs), jnp.stack(nss))
```

```python
import functools
import math

import jax
import jax.numpy as jnp
from jax import lax
from jax.experimental import pallas as pl
from jax.experimental.pallas import tpu as pltpu

F32 = jnp.float32
BF16 = jnp.bfloat16

EPS = 1e-6
DA_HEADS = 8
DA_VDIM = 128
DA_QKDIM = 64
GLA_HEADS = 4
GLA_KDIM = 128
GLA_VDIM = 256
GATE_RANK = 16
GATE_NORMALIZER = 16.0
GLA_CHUNK = 64
LANES = 128

VMEM_LIMIT = 56 * 1024 * 1024


def _params(*sem):
    return pltpu.CompilerParams(dimension_semantics=sem, vmem_limit_bytes=VMEM_LIMIT)


def _tile(n, t):
    t = min(n, t)
    assert n % t == 0, (n, t)
    return t


def _ada_kernel(c_ref, w_ref, b_ref, o_ref):
    c = c_ref[...]
    s = (c * jax.nn.sigmoid(c)).astype(BF16)
    o_ref[...] = jnp.dot(s, w_ref[...].astype(BF16), preferred_element_type=F32) + b_ref[...]


def _adaln(c, ada_w, ada_b):
    r, d = c.shape
    n = ada_w.shape[1]
    tn = _tile(n, 1024)
    return pl.pallas_call(
        _ada_kernel,
        out_shape=jax.ShapeDtypeStruct((r, n), F32),
        grid=(n // tn,),
        in_specs=[
            pl.BlockSpec((r, d), lambda j: (0, 0)),
            pl.BlockSpec((d, tn), lambda j: (0, j)),
            pl.BlockSpec((1, tn), lambda j: (0, j)),
        ],
        out_specs=pl.BlockSpec((r, tn), lambda j: (0, j)),
        compiler_params=_params("parallel"),
        name="adaln",
    )(c, ada_w, ada_b.reshape(1, n))


def _prenorm_kernel(x_ref, g_ref, sc_ref, sh_ref, o_ref):
    x = x_ref[0]
    ms = jnp.mean(x * x, axis=-1, keepdims=True)
    y = x * lax.rsqrt(ms + EPS) * g_ref[...]
    o_ref[0] = (y * (1.0 + sc_ref[0]) + sh_ref[0]).astype(o_ref.dtype)


def _mod_spec(mod, tt, d):
    if mod.shape[1] == 1:
        return pl.BlockSpec((1, 1, d), lambda b, t: (b, 0, 0))
    return pl.BlockSpec((1, tt, d), lambda b, t: (b, t, 0))


def _prenorm(x, g, scale, shift):
    b, t, d = x.shape
    tt = _tile(t, 512)
    return pl.pallas_call(
        _prenorm_kernel,
        out_shape=jax.ShapeDtypeStruct((b, t, d), BF16),
        grid=(b, t // tt),
        in_specs=[
            pl.BlockSpec((1, tt, d), lambda b, t: (b, t, 0)),
            pl.BlockSpec((1, d), lambda b, t: (0, 0)),
            _mod_spec(scale, tt, d),
            _mod_spec(shift, tt, d),
        ],
        out_specs=pl.BlockSpec((1, tt, d), lambda b, t: (b, t, 0)),
        compiler_params=_params("parallel", "parallel"),
        name="prenorm",
    )(x, g.reshape(1, d), scale, shift)


def _proj_kernel(*refs, norm, out_scale, n_out):
    if norm:
        h_ref, w_ref, g_ref, gm_ref = refs[:4]
        outs = refs[4:4 + n_out]
    else:
        h_ref, w_ref = refs[:2]
        outs = refs[2:2 + n_out]
    acc = jnp.dot(h_ref[...], w_ref[...], preferred_element_type=F32)
    if not norm:
        for o in outs:
            o[...] = acc.astype(o.dtype)
        return
    tn = acc.shape[1]
    for cb in range(tn // LANES):
        y = acc[:, cb * LANES:(cb + 1) * LANES]
        ss = jnp.dot((y * y).astype(BF16), gm_ref[...], preferred_element_type=F32)
        yn = y * lax.rsqrt(ss * (1.0 / DA_QKDIM) + EPS) * g_ref[...]
        for o, s in zip(outs, out_scale):
            o[:, cb * LANES:(cb + 1) * LANES] = (yn * s if s != 1.0 else yn).astype(o.dtype)


def _proj(h, w, out_dtypes, *, norm_g=None, out_scale=None):
    m, k = h.shape
    n = w.shape[1]
    tm = _tile(m, 1024)
    tn = _tile(n, 1024)
    norm = norm_g is not None
    n_out = len(out_dtypes)
    if out_scale is None:
        out_scale = (1.0,) * n_out
    in_specs = [
        pl.BlockSpec((tm, k), lambda j, i: (i, 0)),
        pl.BlockSpec((k, tn), lambda j, i: (0, j)),
    ]
    args = [h, w]
    if norm:
        g2 = jnp.concatenate([norm_g, norm_g]).reshape(1, LANES).astype(F32)
        lane = jnp.arange(LANES) // DA_QKDIM
        gm = (lane[:, None] == lane[None, :]).astype(BF16)
        in_specs += [
            pl.BlockSpec((1, LANES), lambda j, i: (0, 0)),
            pl.BlockSpec((LANES, LANES), lambda j, i: (0, 0)),
        ]
        args += [g2, gm]
    outs = pl.pallas_call(
        functools.partial(_proj_kernel, norm=norm, out_scale=tuple(out_scale), n_out=n_out),
        out_shape=[jax.ShapeDtypeStruct((m, n), dt) for dt in out_dtypes],
        grid=(n // tn, m // tm),
        in_specs=in_specs,
        out_specs=[pl.BlockSpec((tm, tn), lambda j, i: (i, j)) for _ in out_dtypes],
        compiler_params=_params("parallel", "parallel"),
        name="proj_norm" if norm else "proj",
    )(*args)
    return outs


def _lambda_value(lq1, lk1, lq2, lk2, lam_init):
    a = jnp.sum(lq1[...] * lk1[...], axis=-1, keepdims=True)
    b = jnp.sum(lq2[...] * lk2[...], axis=-1, keepdims=True)
    return jnp.exp(a) - jnp.exp(b) + lam_init


def _split_halves(q):
    lane = lax.broadcasted_iota(jnp.int32, q.shape, 1)
    zero = jnp.zeros_like(q)
    return jnp.concatenate(
        [jnp.where(lane < DA_QKDIM, q, zero), jnp.where(lane >= DA_QKDIM, q, zero)], axis=0)


def _attn_kernel(q_ref, k_ref, v_ref, lq1, lk1, lq2, lk2, g_ref, o_ref, acc_ref, *, blk, lam_init):
    qi = pl.program_id(2)
    q2 = _split_halves(q_ref[0])
    rows = 2 * blk
    acc_ref[...] = jnp.zeros_like(acc_ref)

    def step(ki, carry, masked):
        m, l = carry
        start = pl.multiple_of(ki * blk, blk)
        k = k_ref[0, pl.ds(start, blk), :]
        v = v_ref[0, pl.ds(start, blk), :]
        s = lax.dot_general(q2, k, (((1,), (1,)), ((), ())), preferred_element_type=F32)
        if masked:
            row = lax.broadcasted_iota(jnp.int32, (rows, blk), 0)
            row = jnp.where(row >= blk, row - blk, row)
            col = lax.broadcasted_iota(jnp.int32, (rows, blk), 1)
            s = jnp.where(col <= row, s, -jnp.inf)
        m_new = jnp.maximum(m, jnp.max(s, axis=1, keepdims=True))
        alpha = jnp.exp(m - m_new)
        p = jnp.exp(s - m_new)
        l = alpha * l + jnp.sum(p, axis=1, keepdims=True)
        acc_ref[...] = acc_ref[...] * alpha + jnp.dot(p.astype(BF16), v, preferred_element_type=F32)
        return m_new, l

    m0 = jnp.full((rows, 1), -jnp.inf, F32)
    l0 = jnp.zeros((rows, 1), F32)
    m, l = lax.fori_loop(0, qi, functools.partial(step, masked=False), (m0, l0))
    m, l = step(qi, (m, l), True)

    o = acc_ref[...] / l
    lam = _lambda_value(lq1, lk1, lq2, lk2, lam_init)
    od = o[:blk] - lam * o[blk:]
    ms = jnp.mean(od * od, axis=-1, keepdims=True)
    o_ref[0] = (od * lax.rsqrt(ms + EPS) * g_ref[...] * (1.0 - lam_init)).astype(o_ref.dtype)


def _attn_prompt(q, k, v, lams, subln_g, lam_init):
    b, t, w = q.shape
    h = w // DA_VDIM
    blk = _tile(t, 512)
    vec = pl.BlockSpec((1, DA_QKDIM), lambda b, h, i: (0, 0))
    return pl.pallas_call(
        functools.partial(_attn_kernel, blk=blk, lam_init=lam_init),
        out_shape=jax.ShapeDtypeStruct((b, t, w), BF16),
        grid=(b, h, t // blk),
        in_specs=[
            pl.BlockSpec((1, blk, DA_VDIM), lambda b, h, i: (b, i, h)),
            pl.BlockSpec((1, t, DA_VDIM), lambda b, h, i: (b, 0, h)),
            pl.BlockSpec((1, t, DA_VDIM), lambda b, h, i: (b, 0, h)),
            vec, vec, vec, vec,
            pl.BlockSpec((1, DA_VDIM), lambda b, h, i: (0, 0)),
        ],
        out_specs=pl.BlockSpec((1, blk, DA_VDIM), lambda b, h, i: (b, i, h)),
        scratch_shapes=[pltpu.VMEM((2 * blk, DA_VDIM), F32)],
        compiler_params=_params("parallel", "parallel", "arbitrary"),
        name="diff_attn_prompt",
    )(q, k, v, *[x.reshape(1, DA_QKDIM) for x in lams], subln_g.reshape(1, DA_VDIM))


def _log_sigmoid(z):
    return jnp.minimum(z, 0.0) - jnp.log1p(jnp.exp(-jnp.abs(z)))


def _row_to_col(row):
    n = row.shape[1]
    r = lax.broadcasted_iota(jnp.int32, (n, n), 0)
    c = lax.broadcasted_iota(jnp.int32, (n, n), 1)
    return jnp.sum(jnp.where(r == c, jnp.broadcast_to(row, (n, n)), 0.0), axis=1, keepdims=True)


def _gate_log(glr, w2, b):
    z = jnp.dot(glr, w2, preferred_element_type=F32, precision=lax.Precision.HIGHEST) + b
    return _log_sigmoid(z) * (1.0 / GATE_NORMALIZER)


def _gla_kernel(gq_ref, gk_ref, gv_ref, go_ref, glr_ref, w2_ref, b_ref, ng_ref,
                o_ref, sfin_ref, s_ref, *, tt):
    ti = pl.program_id(2)

    @pl.when(ti == 0)
    def _():
        s_ref[...] = jnp.zeros_like(s_ref)

    c = GLA_CHUNK
    g_all = _gate_log(glr_ref[0], w2_ref[...], b_ref[...])
    r = lax.broadcasted_iota(jnp.int32, (c, c), 0)
    cc = lax.broadcasted_iota(jnp.int32, (c, c), 1)
    tril = r >= cc
    tril_f = tril.astype(F32)
    for ci in range(tt // c):
        sl = slice(ci * c, (ci + 1) * c)
        g = g_all[sl]
        bc = jnp.dot(tril_f, g, preferred_element_type=F32, precision=lax.Precision.HIGHEST)
        b_last = bc[c - 1:c]
        q = gq_ref[0, sl, :].astype(F32) * (GLA_KDIM ** -0.5)
        k = gk_ref[0, sl, :].astype(F32)
        v = gv_ref[0, sl, :]
        q_t = (q * jnp.exp(bc)).astype(BF16)
        k_t = (k * jnp.exp(-bc)).astype(BF16)
        k_e = (k * jnp.exp(b_last - bc)).astype(BF16)
        att = lax.dot_general(q_t, k_t, (((1,), (1,)), ((), ())), preferred_element_type=F32)
        att = jnp.where(tril, att, 0.0).astype(BF16)
        s_prev = s_ref[...]
        o = (jnp.dot(att, v, preferred_element_type=F32)
             + jnp.dot(q_t, s_prev.astype(BF16), preferred_element_type=F32))
        kv = lax.dot_general(k_e, v, (((0,), (0,)), ((), ())), preferred_element_type=F32)
        s_ref[...] = _row_to_col(jnp.exp(b_last)) * s_prev + kv
        ms = jnp.mean(o * o, axis=-1, keepdims=True)
        gate = go_ref[0, sl, :].astype(F32)
        gate = gate * jax.nn.sigmoid(gate)
        o_ref[0, sl, :] = (o * lax.rsqrt(ms + EPS) * ng_ref[...] * gate).astype(o_ref.dtype)

    @pl.when(ti == pl.num_programs(2) - 1)
    def _():
        sfin_ref[0, 0] = s_ref[...]


def _gla_prompt(gall, glr, w2p, gate_b, norm_g):
    b, t, _ = gall.shape
    tt = _tile(t, 512)
    hk = GLA_HEADS
    return pl.pallas_call(
        functools.partial(_gla_kernel, tt=tt),
        out_shape=[
            jax.ShapeDtypeStruct((b, t, GLA_HEADS * GLA_VDIM), BF16),
            jax.ShapeDtypeStruct((b, GLA_HEADS, GLA_KDIM, GLA_VDIM), F32),
        ],
        grid=(b, GLA_HEADS, t // tt),
        in_specs=[
            pl.BlockSpec((1, tt, GLA_KDIM), lambda b, h, i: (b, i, h)),
            pl.BlockSpec((1, tt, GLA_KDIM), lambda b, h, i: (b, i, hk + h)),
            pl.BlockSpec((1, tt, GLA_VDIM), lambda b, h, i: (b, i, hk + h)),
            pl.BlockSpec((1, tt, GLA_VDIM), lambda b, h, i: (b, i, 2 * hk + h)),
            pl.BlockSpec((1, tt, LANES), lambda b, h, i: (b, i, 0)),
            pl.BlockSpec((LANES, GLA_KDIM), lambda b, h, i: (0, h)),
            pl.BlockSpec((1, GLA_KDIM), lambda b, h, i: (0, h)),
            pl.BlockSpec((1, GLA_VDIM), lambda b, h, i: (0, 0)),
        ],
        out_specs=[
            pl.BlockSpec((1, tt, GLA_VDIM), lambda b, h, i: (b, i, h)),
            pl.BlockSpec((1, 1, GLA_KDIM, GLA_VDIM), lambda b, h, i: (b, h, 0, 0)),
        ],
        scratch_shapes=[pltpu.VMEM((GLA_KDIM, GLA_VDIM), F32)],
        compiler_params=_params("parallel", "parallel", "arbitrary"),
        name="gla_prompt",
    )(gall, gall, gall, gall, glr, w2p, gate_b.reshape(1, -1), norm_g.reshape(1, GLA_VDIM))


def _outproj_kernel(ma_ref, mb_ref, wa_ref, wb_ref, x_ref, g1_ref, ng_ref, sc_ref, sh_ref,
                    x1_ref, h2_ref):
    y = (jnp.dot(ma_ref[0], wa_ref[...], preferred_element_type=F32)
         + jnp.dot(mb_ref[0], wb_ref[...], preferred_element_type=F32))
    x1 = x_ref[0] + g1_ref[0] * y
    x1_ref[0] = x1
    ms = jnp.mean(x1 * x1, axis=-1, keepdims=True)
    h = x1 * lax.rsqrt(ms + EPS) * ng_ref[...]
    h2_ref[0] = (h * (1.0 + sc_ref[0]) + sh_ref[0]).astype(h2_ref.dtype)


def _outproj(mix_a, mix_b, w_a, w_b, x, gate1, norm2_g, scale2, shift2):
    b, t, d = x.shape
    ka = mix_a.shape[2]
    kb = mix_b.shape[2]
    tt = _tile(t, 256)
    row = lambda b, t: (b, t, 0)
    const = lambda b, t: (0, 0)
    return pl.pallas_call(
        _outproj_kernel,
        out_shape=[jax.ShapeDtypeStruct((b, t, d), F32), jax.ShapeDtypeStruct((b, t, d), BF16)],
        grid=(b, t // tt),
        in_specs=[
            pl.BlockSpec((1, tt, ka), row),
            pl.BlockSpec((1, tt, kb), row),
            pl.BlockSpec((ka, d), const),
            pl.BlockSpec((kb, d), const),
            pl.BlockSpec((1, tt, d), row),
            _mod_spec(gate1, tt, d),
            pl.BlockSpec((1, d), const),
            _mod_spec(scale2, tt, d),
            _mod_spec(shift2, tt, d),
        ],
        out_specs=[pl.BlockSpec((1, tt, d), row), pl.BlockSpec((1, tt, d), row)],
        compiler_params=_params("parallel", "parallel"),
        name="outproj",
    )(mix_a, mix_b, w_a, w_b, x, gate1, norm2_g.reshape(1, d), scale2, shift2)


def _mlp_kernel(h_ref, wu_ref, wd_ref, x_ref, g2_ref, o_ref, acc_ref):
    f = pl.program_id(2)

    @pl.when(f == 0)
    def _():
        acc_ref[...] = jnp.zeros_like(acc_ref)

    u = jnp.dot(h_ref[0], wu_ref[...], preferred_element_type=F32)
    u = jnp.maximum(u, 0.0)
    u = (u * u).astype(BF16)
    acc_ref[...] += jnp.dot(u, wd_ref[...], preferred_element_type=F32)

    @pl.when(f == pl.num_programs(2) - 1)
    def _():
        o_ref[0] = x_ref[0] + g2_ref[0] * acc_ref[...]


def _mlp(h2, w_up, w_down, x1, gate2):
    b, t, d = x1.shape
    ff = w_up.shape[1]
    tt = _tile(t, 512)
    tf = _tile(ff, 1024)
    if gate2.shape[1] == 1:
        g_spec = pl.BlockSpec((1, 1, d), lambda b, t, f: (b, 0, 0))
    else:
        g_spec = pl.BlockSpec((1, tt, d), lambda b, t, f: (b, t, 0))
    row = lambda b, t, f: (b, t, 0)
    return pl.pallas_call(
        _mlp_kernel,
        out_shape=jax.ShapeDtypeStruct((b, t, d), F32),
        grid=(b, t // tt, ff // tf),
        in_specs=[
            pl.BlockSpec((1, tt, d), row),
            pl.BlockSpec((d, tf), lambda b, t, f: (0, f)),
            pl.BlockSpec((tf, d), lambda b, t, f: (f, 0)),
            pl.BlockSpec((1, tt, d), row),
            g_spec,
        ],
        out_specs=pl.BlockSpec((1, tt, d), row),
        scratch_shapes=[pltpu.VMEM((tt, d), F32)],
        compiler_params=_params("parallel", "parallel", "arbitrary"),
        name="mlp",
    )(h2, w_up, w_down, x1, gate2)


def _decode_kernel(pt_ref, q_ref, kn_ref, vn_ref, ck_ref, cv_ref, lq1, lk1, lq2, lk2, g_ref,
                   o_ref, m_ref, l_ref, acc_ref, *, lam_init):
    del pt_ref
    p = pl.program_id(1)
    nh = DA_HEADS
    page = ck_ref.shape[0]
    cols = page * nh

    @pl.when(p == 0)
    def _():
        m_ref[...] = jnp.full_like(m_ref, -jnp.inf)
        l_ref[...] = jnp.zeros_like(l_ref)
        acc_ref[...] = jnp.zeros_like(acc_ref)

    q2 = _split_halves(q_ref[0]).astype(BF16)
    kp =ck_ref[...].reshape(cols, DA_VDIM).astype(BF16)
    vp = cv_ref[...].reshape(cols, DA_VDIM).astype(BF16)
    s = lax.dot_general(q2, kp, (((1,), (1,)), ((), ())), preferred_element_type=F32)
    row_h = lax.broadcasted_iota(jnp.int32, (2 * nh, cols), 0) % nh
    col_h = lax.broadcasted_iota(jnp.int32, (2 * nh, cols), 1) % nh
    s = jnp.where(row_h == col_h, s, -jnp.inf)
    m_prev = m_ref[...]
    m_new = jnp.maximum(m_prev, jnp.max(s, axis=1, keepdims=True))
    alpha = jnp.exp(m_prev - m_new)
    pr = jnp.exp(s - m_new)
    l_ref[...] = alpha * l_ref[...] + jnp.sum(pr, axis=1, keepdims=True)
    acc_ref[...] = acc_ref[...] * alpha + jnp.dot(pr.astype(BF16), vp, preferred_element_type=F32)
    m_ref[...] = m_new

    @pl.when(p == pl.num_programs(1) - 1)
    def _():
        kn = kn_ref[0]
        vn = vn_ref[0]
        k2 = jnp.concatenate([kn, kn], axis=0)
        v2 = jnp.concatenate([vn, vn], axis=0)
        s_new = jnp.sum(q2.astype(F32) * k2, axis=1, keepdims=True)
        m_old = m_ref[...]
        m_fin = jnp.maximum(m_old, s_new)
        a = jnp.exp(m_old - m_fin)
        pn = jnp.exp(s_new - m_fin)
        l = a * l_ref[...] + pn
        o = (acc_ref[...] * a + pn * v2) / l
        lam = _lambda_value(lq1, lk1, lq2, lk2, lam_init)
        od = o[:nh] - lam * o[nh:]
        ms = jnp.mean(od * od, axis=-1, keepdims=True)
        o_ref[0] = od * lax.rsqrt(ms + EPS) * g_ref[...] * (1.0 - lam_init)


def _attn_sample(q, k_new, v_new, cache_k, cache_v, page_table, lams, subln_g, lam_init):
    db, nh, dv = q.shape
    n_pages = page_table.shape[1]
    page = cache_k.shape[1]
    tok = pl.BlockSpec((1, nh, dv), lambda b, p, pt: (b, 0, 0))
    cache = pl.BlockSpec((None, page, nh, dv), lambda b, p, pt: (pt[b, p], 0, 0, 0))
    vec = pl.BlockSpec((1, DA_QKDIM), lambda b, p, pt: (0, 0))
    grid_spec = pltpu.PrefetchScalarGridSpec(
        num_scalar_prefetch=1,
        grid=(db, n_pages),
        in_specs=[tok, tok, tok, cache, cache, vec, vec, vec, vec,
                  pl.BlockSpec((1, dv), lambda b, p, pt: (0, 0))],
        out_specs=pl.BlockSpec((1, nh, dv), lambda b, p, pt: (b, 0, 0)),
        scratch_shapes=[
            pltpu.VMEM((2 * nh, 1), F32),
            pltpu.VMEM((2 * nh, 1), F32),
            pltpu.VMEM((2 * nh, dv), F32),
        ],
    )
    return pl.pallas_call(
        functools.partial(_decode_kernel, lam_init=lam_init),
        out_shape=jax.ShapeDtypeStruct((db, nh, dv), F32),
        grid_spec=grid_spec,
        compiler_params=_params("parallel", "arbitrary"),
        name="diff_attn_sample",
    )(page_table, q, k_new, v_new, cache_k, cache_v,
      *[x.reshape(1, DA_QKDIM) for x in lams], subln_g.reshape(1, dv))


def _gla_step_kernel(g_ref, glr_ref, w2_ref, b_ref, ng_ref, s0_ref, o_ref, s_ref):
    hk = GLA_HEADS * GLA_KDIM
    glr = jnp.broadcast_to(glr_ref[0], (8, LANES))
    g_all = _gate_log(glr, w2_ref[...], b_ref[...])[0:1]
    for h in range(GLA_HEADS):
        ks = slice(h * GLA_KDIM, (h + 1) * GLA_KDIM)
        q = g_ref[0, :, ks].astype(F32) * (GLA_KDIM ** -0.5)
        k = g_ref[0, :, hk + h * GLA_KDIM:hk + (h + 1) * GLA_KDIM].astype(F32)
        v = g_ref[0, :, 2 * hk + h * GLA_VDIM:2 * hk + (h + 1) * GLA_VDIM].astype(F32)
        gate = g_ref[0, :, 2 * hk + (GLA_HEADS + h) * GLA_VDIM:
                     2 * hk + (GLA_HEADS + h + 1) * GLA_VDIM].astype(F32)
        s = _row_to_col(jnp.exp(g_all[:, ks])) * s0_ref[h] + _row_to_col(k) * v
        s_ref[h] = s
        o = jnp.sum(_row_to_col(q) * s, axis=0, keepdims=True)
        ms = jnp.mean(o * o, axis=-1, keepdims=True)
        o_ref[0, :, h * GLA_VDIM:(h + 1) * GLA_VDIM] = (
            o * lax.rsqrt(ms + EPS) * ng_ref[...] * (gate * jax.nn.sigmoid(gate)))


def _gla_sample(gall, glr, w2p, gate_b, norm_g, state):
    db = gall.shape[0]
    wg = gall.shape[2]
    const = lambda b: (0, 0)
    return pl.pallas_call(
        _gla_step_kernel,
        out_shape=[
            jax.ShapeDtypeStruct((db, 1, GLA_HEADS * GLA_VDIM), F32),
            jax.ShapeDtypeStruct(state.shape, F32),
        ],
        grid=(db,),
        in_specs=[
            pl.BlockSpec((1, 1, wg), lambda b: (b, 0, 0)),
            pl.BlockSpec((1, 1, LANES), lambda b: (b, 0, 0)),
            pl.BlockSpec((LANES, GLA_HEADS * GLA_KDIM), const),
            pl.BlockSpec((1, GLA_HEADS * GLA_KDIM), const),
            pl.BlockSpec((1, GLA_VDIM), const),
            pl.BlockSpec((None, GLA_HEADS, GLA_KDIM, GLA_VDIM), lambda b: (b, 0, 0, 0)),
        ],
        out_specs=[
            pl.BlockSpec((1, 1, GLA_HEADS * GLA_VDIM), lambda b: (b, 0, 0)),
            pl.BlockSpec((None, GLA_HEADS, GLA_KDIM, GLA_VDIM), lambda b: (b, 0, 0, 0)),
        ],
        compiler_params=_params("parallel"),
        name="gla_sample",
    )(gall, glr, w2p, gate_b.reshape(1, -1), norm_g.reshape(1, GLA_VDIM), state)


def _layer_weights(w_in, gate_w2, w_out, w_up, w_down):
    da = DA_HEADS * DA_VDIM
    gk = GLA_HEADS * GLA_KDIM
    gv = GLA_HEADS * GLA_VDIM
    c = [0, da, 2 * da, 3 * da, 3 * da + gk, 3 * da + 2 * gk, 3 * da + 2 * gk + gv]
    c_glr = c[6]
    c_gout = c_glr + GATE_RANK
    wb = w_in.astype(BF16)
    w_gall = jnp.concatenate([wb[:, c[3]:c[6]], wb[:, c_gout:c_gout + gv]], axis=1)
    w_glr = jnp.pad(wb[:, c_glr:c_gout], ((0, 0), (0, LANES - GATE_RANK)))
    w2p = jnp.pad(gate_w2.astype(F32), ((0, LANES - GATE_RANK), (0, 0)))
    wo = w_out.astype(BF16)
    return dict(
        wq=wb[:, c[0]:c[1]], wk=wb[:, c[1]:c[2]], wv=wb[:, c[2]:c[3]],
        w_gall=w_gall, w_glr=w_glr, w2p=w2p,
        wo_a=wo[:da], wo_b=wo[da:], w_up=w_up.astype(BF16), w_down=w_down.astype(BF16))


def _mixer_inputs(x, scale1, shift1, norm1_g, lw, q_norm_g, k_norm_g):
    b, t, d = x.shape
    h = _prenorm(x, norm1_g, scale1, shift1).reshape(b * t, d)
    (q,) = _proj(h, lw["wq"], [BF16], norm_g=q_norm_g, out_scale=(DA_QKDIM ** -0.5,))
    k32, k16 = _proj(h, lw["wk"], [F32, BF16], norm_g=k_norm_g)
    v32, v16 = _proj(h, lw["wv"], [F32, BF16])
    (gall,) = _proj(h, lw["w_gall"], [BF16])
    (glr,) = _proj(h, lw["w_glr"], [F32])
    return q, k32, k16, v32, v16, gall, glr


def kernel(x_prompt, x_sample, cache_k, cache_v, state_gla, page_table, c_prompt, c_sample, norm1_g, ada_w, ada_b, w_in, q_norm_g, k_norm_g, lambda_q1, lambda_k1, lambda_q2, lambda_k2, diff_subln_g, gla_gate_w2, gla_gate_b, gla_norm_g, w_out, norm2_g, w_up, w_down):
    depth = w_in.shape[0]
    bp, tp, d = x_prompt.shape
    db, ts, _ = x_sample.shape
    assert ts == 1, "sample group decodes one token per sequence"
    da = DA_HEADS * DA_VDIM
    xp = x_prompt
    xs = x_sample.reshape(1, db, d)
    outs = [[] for _ in range(6)]
    n_ctl = bp + db
    n_pad = -n_ctl % 16
    c_all = jnp.pad(jnp.concatenate([c_prompt, c_sample], axis=0), ((0, n_pad), (0, 0)))
    for i in range(depth):
        lam_init = 0.8 - 0.6 * math.exp(-0.3 * i)
        lams = (lambda_q1[i], lambda_k1[i], lambda_q2[i], lambda_k2[i])
        lw = _layer_weights(w_in[i], gla_gate_w2[i], w_out[i], w_up[i], w_down[i])
        mod = _adaln(c_all, ada_w[i], ada_b[i])
        mods_p = [m[:bp, None, :] for m in jnp.split(mod, 6, axis=-1)]
        mods_s = [m[None, bp:bp + db, :] for m in jnp.split(mod, 6, axis=-1)]

        sh1, sc1, g1, sh2, sc2, g2 = mods_p
        q, k32, k16, v32, v16, gall, glr = _mixer_inputs(xp, sc1, sh1, norm1_g[i], lw, q_norm_g[i], k_norm_g[i])
        o_da = _attn_prompt(q.reshape(bp, tp, da), k16.reshape(bp, tp, da), v16.reshape(bp, tp, da),
                            lams, diff_subln_g[i], lam_init)
        o_gla, s_p = _gla_prompt(gall.reshape(bp, tp, -1), glr.reshape(bp, tp, LANES), lw["w2p"],
                                 gla_gate_b[i], gla_norm_g[i])
        outs[0].append(k32.reshape(bp, tp, DA_HEADS, DA_VDIM))
        outs[1].append(v32.reshape(bp, tp, DA_HEADS, DA_VDIM))
        outs[2].append(s_p)
        x1, h2 = _outproj(o_da, o_gla, lw["wo_a"], lw["wo_b"], xp, g1, norm2_g[i], sc2, sh2)
        xp = _mlp(h2, lw["w_up"], lw["w_down"], x1, g2)

        sh1, sc1, g1, sh2, sc2, g2 = mods_s
        q, k32, k16, v32, v16, gall, glr = _mixer_inputs(xs, sc1, sh1, norm1_g[i], lw, q_norm_g[i], k_norm_g[i])
        o_da = _attn_sample(q.astype(F32).reshape(db, DA_HEADS, DA_VDIM),k32.reshape(db, DA_HEADS, DA_VDIM),
                            v32.reshape(db, DA_HEADS, DA_VDIM), cache_k[i], cache_v[i], page_table,
                            lams, diff_subln_g[i], lam_init)
        o_gla, s_s = _gla_sample(gall.reshape(db, 1, -1), glr.reshape(db, 1, LANES), lw["w2p"],
                                 gla_gate_b[i], gla_norm_g[i], state_gla[i])
        outs[3].append(k32.reshape(db, 1, DA_HEADS, DA_VDIM))
        outs[4].append(v32.reshape(db, 1, DA_HEADS, DA_VDIM))
        outs[5].append(s_s)
        x1, h2 = _outproj(o_da.reshape(1, db, da).astype(BF16), o_gla.reshape(1, db, -1).astype(BF16),
                          lw["wo_a"], lw["wo_b"], xs, g1, norm2_g[i], sc2, sh2)
        xs = _mlp(h2, lw["w_up"], lw["w_down"], x1, g2)

    return (xp, xs.reshape(db, 1, d), *[jnp.stack(o) for o in outs])
```

```python
import functools
import math

import jax
import jax.numpy as jnp
from jax import lax
from jax.experimental import pallas as pl
from jax.experimental.pallas import tpu as pltpu

F32 = jnp.float32
BF16 = jnp.bfloat16

EPS = 1e-6
DA_HEADS = 8
DA_VDIM = 128
DA_QKDIM = 64
GLA_HEADS = 4
GLA_KDIM = 128
GLA_VDIM = 256
GATE_RANK = 16
GATE_NORMALIZER = 16.0
GLA_CHUNK = 64
LANES = 128
Q_SCALE = DA_QKDIM ** -0.5 * math.log2(math.e)
DECODE_PAGES_PER_STEP = 8
ATTN_ROW_SUB = 256

VMEM_LIMIT = 56 * 1024 * 1024


def _params(*sem):
    return pltpu.CompilerParams(dimension_semantics=sem, vmem_limit_bytes=VMEM_LIMIT)


def _tile(n, t):
    t = min(n, t)
    assert n % t == 0, (n, t)
    return t


def _ada_kernel(c_ref, w_ref, b_ref, o_ref):
    c = c_ref[...]
    s = (c * jax.nn.sigmoid(c)).astype(BF16)
    o_ref[...] = jnp.dot(s, w_ref[...].astype(BF16), preferred_element_type=F32) + b_ref[...]


def _adaln(c, ada_w, ada_b):
    r, d = c.shape
    n = ada_w.shape[1]
    tn = _tile(n, 1024)
    return pl.pallas_call(
        _ada_kernel,
        out_shape=jax.ShapeDtypeStruct((r, n), F32),
        grid=(n // tn,),
        in_specs=[
            pl.BlockSpec((r, d), lambda j: (0, 0)),
            pl.BlockSpec((d, tn), lambda j: (0, j)),
            pl.BlockSpec((1, tn), lambda j: (0, j)),
        ],
        out_specs=pl.BlockSpec((r, tn), lambda j: (0, j)),
        compiler_params=_params("parallel"),
        name="adaln",
    )(c, ada_w, ada_b.reshape(1, n))


def _prenorm_kernel(x_ref, g_ref, sc_ref, sh_ref, o_ref):
    x = x_ref[0]
    ms = jnp.mean(x * x, axis=-1, keepdims=True)
    y = x * lax.rsqrt(ms + EPS) * g_ref[...]
    o_ref[0] = (y * (1.0 + sc_ref[0]) + sh_ref[0]).astype(o_ref.dtype)


def _mod_spec(mod, tt, d):
    if mod.shape[1] == 1:
        return pl.BlockSpec((1, 1, d), lambda b, t: (b, 0, 0))
    return pl.BlockSpec((1, tt, d), lambda b, t: (b, t, 0))


def _prenorm(x, g, scale, shift):
    b, t, d = x.shape
    tt = _tile(t, 512)
    return pl.pallas_call(
        _prenorm_kernel,
        out_shape=jax.ShapeDtypeStruct((b, t, d), BF16),
        grid=(b, t // tt),
        in_specs=[
            pl.BlockSpec((1, tt, d), lambda b, t: (b, t, 0)),
            pl.BlockSpec((1, d), lambda b, t: (0, 0)),
            _mod_spec(scale, tt, d),
            _mod_spec(shift, tt, d),
        ],
        out_specs=pl.BlockSpec((1, tt, d), lambda b, t: (b, t, 0)),
        compiler_params=_params("parallel", "parallel"),
        name="prenorm",
    )(x, g.reshape(1, d), scale, shift)


def _proj_kernel(*refs, norm, out_scale, n_out):
    if norm:
        h_ref, w_ref, g_ref, gm_ref = refs[:4]
        outs = refs[4:4 + n_out]
    else:
        h_ref, w_ref = refs[:2]
        outs = refs[2:2 + n_out]
    acc = jnp.dot(h_ref[...], w_ref[...], preferred_element_type=F32)
    if not norm:
        for o in outs:
            o[...] = acc.astype(o.dtype)
        return
    tn = acc.shape[1]
    for cb in range(tn // LANES):
        y = acc[:, cb * LANES:(cb + 1) * LANES]
        ss = jnp.dot((y * y).astype(BF16), gm_ref[...], preferred_element_type=F32)
        yn = y * lax.rsqrt(ss * (1.0 / DA_QKDIM) + EPS) * g_ref[...]
        for o, s in zip(outs, out_scale):
            o[:, cb * LANES:(cb + 1) * LANES] = (yn * s if s != 1.0 else yn).astype(o.dtype)


def _proj(h, w, out_dtypes, *, norm_g=None, out_scale=None):
    m, k = h.shape
    n = w.shape[1]
    tm = _tile(m, 1024)
    tn = _tile(n, 1024)
    norm = norm_g is not None
    n_out = len(out_dtypes)
    if out_scale is None:
        out_scale = (1.0,) * n_out
    in_specs = [
        pl.BlockSpec((tm, k), lambda j, i: (i, 0)),
        pl.BlockSpec((k, tn), lambda j, i: (0, j)),
    ]
    args = [h, w]
    if norm:
        g2 = jnp.concatenate([norm_g, norm_g]).reshape(1, LANES).astype(F32)
        lane = jnp.arange(LANES) // DA_QKDIM
        gm = (lane[:, None] == lane[None, :]).astype(BF16)
        in_specs += [
            pl.BlockSpec((1, LANES), lambda j, i: (0, 0)),
            pl.BlockSpec((LANES, LANES), lambda j, i: (0, 0)),
        ]
        args += [g2, gm]
    outs = pl.pallas_call(
        functools.partial(_proj_kernel, norm=norm, out_scale=tuple(out_scale), n_out=n_out),
        out_shape=[jax.ShapeDtypeStruct((m, n), dt) for dt in out_dtypes],
        grid=(n // tn, m // tm),
        in_specs=in_specs,
        out_specs=[pl.BlockSpec((tm, tn), lambda j, i: (i, j)) for _ in out_dtypes],
        compiler_params=_params("parallel", "parallel"),
        name="proj_norm" if norm else "proj",
    )(*args)
    return outs


def _lambda_value(lq1, lk1, lq2, lk2, lam_init):
    a = jnp.sum(lq1[...] * lk1[...], axis=-1, keepdims=True)
    b = jnp.sum(lq2[...] * lk2[...], axis=-1, keepdims=True)
    return jnp.exp(a) - jnp.exp(b) + lam_init


def _split_halves(q):
    lane = lax.broadcasted_iota(jnp.int32, q.shape, 1)
    zero = jnp.zeros_like(q)
    return jnp.concatenate(
        [jnp.where(lane < DA_QKDIM, q, zero), jnp.where(lane >= DA_QKDIM, q, zero)], axis=0)


def _attn_kernel(q_ref, k_ref, v_ref, lq1, lk1, lq2, lk2, g_ref, o_ref,
                 vx_ref, q2_ref, m_ref, acc_ref, sa_ref, sb_ref, *, blk, sub, lam_init):
    qi = pl.program_id(2)
    rows = 2 * blk

    @pl.when(qi == 0)
    def _():
        vx_ref[:, :DA_VDIM] = v_ref[0]
        vx_ref[:, DA_VDIM:] = jnp.ones((vx_ref.shape[0], DA_VDIM), vx_ref.dtype)

    q2_ref[...] = _split_halves(q_ref[0])
    m_ref[...] = jnp.full_like(m_ref, -jnp.inf)
    acc_ref[...] = jnp.zeros_like(acc_ref)

    def scores(ki, s_ref):
        start = pl.multiple_of(ki * blk, blk)
        k = k_ref[0, pl.ds(start, blk), :]
        for r in range(rows // sub):
            rs = slice(r * sub, (r + 1) * sub)
            s_ref[rs] = lax.dot_general(q2_ref[rs], k, (((1,), (1,)), ((), ())),
                                        preferred_element_type=F32)

    def update(ki, s_ref, masked):
        start = pl.multiple_of(ki * blk, blk)
        for r in range(rows // sub):
            rs = slice(r * sub, (r + 1) * sub)
            q_lo = (r * sub) % blk
            ncol = q_lo + sub if masked else blk
            vx = vx_ref[pl.ds(start, ncol), :]
            s = s_ref[rs, :ncol]
            if masked:
                row = q_lo + lax.broadcasted_iota(jnp.int32, (sub, ncol), 0)
                col = lax.broadcasted_iota(jnp.int32, (sub, ncol), 1)
                s = jnp.where(col <= row, s, -jnp.inf)
            m_prev = m_ref[rs]
            m_new = jnp.maximum(m_prev, jnp.max(s, axis=1, keepdims=True))
            alpha = jnp.exp2(m_prev - m_new)
            p = jnp.exp2(s - pltpu.repeat(m_new, ncol // LANES, axis=1)).astype(BF16)
            acc_ref[rs] = (acc_ref[rs] * pltpu.repeat(alpha, 2, axis=1)
                           + jnp.dot(p, vx, preferred_element_type=F32))
            m_ref[rs] = m_new

    scores(0, sa_ref)

    def pair(j, carry):
        scores(2 * j + 1, sb_ref)
        update(2 * j, sa_ref, False)
        scores(2 * j + 2, sa_ref)
        update(2 * j + 1, sb_ref, False)
        return carry

    lax.fori_loop(0, qi // 2, pair, 0)

    @pl.when(qi % 2 == 0)
    def _():
        update(qi, sa_ref, True)

    @pl.when(qi % 2 == 1)
    def _():
        scores(qi, sb_ref)
        update(qi - 1, sa_ref, False)
        update(qi, sb_ref, True)

    acc = acc_ref[...]
    o = acc[:, :DA_VDIM] / acc[:, DA_VDIM:]
    lam = _lambda_value(lq1, lk1, lq2, lk2, lam_init)
    od = o[:blk] - lam * o[blk:]
    ms = jnp.mean(od * od, axis=-1, keepdims=True)
    o_ref[0] = (od * lax.rsqrt(ms + EPS) * g_ref[...] * (1.0 - lam_init)).astype(o_ref.dtype)


def _attn_prompt(q, k, v, lams, subln_g, lam_init):
    b, t, w = q.shape
    h = w // DA_VDIM
    blk = _tile(t, 512)
    vec = pl.BlockSpec((1, DA_QKDIM), lambda b, h, i: (0, 0))
    return pl.pallas_call(
        functools.partial(_attn_kernel, blk=blk, sub=_tile(blk, ATTN_ROW_SUB), lam_init=lam_init),
        out_shape=jax.ShapeDtypeStruct((b, t, w), BF16),
        grid=(b, h, t // blk),
        in_specs=[
            pl.BlockSpec((1, blk, DA_VDIM), lambda b, h, i: (b, i, h)),
            pl.BlockSpec((1, t, DA_VDIM), lambda b, h, i: (b, 0, h)),
            pl.BlockSpec((1, t, DA_VDIM), lambda b, h, i: (b, 0, h)),
            vec, vec, vec, vec,
            pl.BlockSpec((1, DA_VDIM), lambda b, h, i: (0, 0)),
        ],
        out_specs=pl.BlockSpec((1, blk, DA_VDIM), lambda b, h, i: (b, i, h)),
        scratch_shapes=[
            pltpu.VMEM((t, 2 * DA_VDIM), BF16),
            pltpu.VMEM((2 * blk, DA_VDIM), BF16),
            pltpu.VMEM((2 * blk, LANES), F32),
            pltpu.VMEM((2 * blk, 2 * DA_VDIM), F32),
            pltpu.VMEM((2 * blk, blk), F32),
            pltpu.VMEM((2 * blk, blk), F32),
        ],
        compiler_params=_params("parallel", "parallel", "arbitrary"),
        name="diff_attn_prompt",
    )(q, k, v, *[x.reshape(1, DA_QKDIM) for x in lams], subln_g.reshape(1, DA_VDIM))


def _log_sigmoid(z):
    return jnp.minimum(z, 0.0) - jnp.log1p(jnp.exp(-jnp.abs(z)))


def _row_to_col(row):
    n = row.shape[1]
    r = lax.broadcasted_iota(jnp.int32, (n, n), 0)
    c = lax.broadcasted_iota(jnp.int32, (n, n), 1)
    return jnp.sum(jnp.where(r == c, jnp.broadcast_to(row, (n, n)), 0.0), axis=1, keepdims=True)


def _gate_log(glr, w2, b):
    z = jnp.dot(glr, w2, preferred_element_type=F32, precision=lax.Precision.HIGHEST) + b
    return _log_sigmoid(z) * (1.0 / GATE_NORMALIZER)


def _gla_kernel(gq_ref, gk_ref, gv_ref, go_ref, glr_ref, w2_ref, b_ref, ng_ref,
                o_ref, sfin_ref, s_ref, *, tt):
    ti = pl.program_id(2)

    @pl.when(ti == 0)
    def _():
        s_ref[...] = jnp.zeros_like(s_ref)

    c = GLA_CHUNK
    g_all = _gate_log(glr_ref[0], w2_ref[...], b_ref[...])
    r = lax.broadcasted_iota(jnp.int32, (c, c), 0)
    cc = lax.broadcasted_iota(jnp.int32, (c, c), 1)
    tril = r >= cc
    tril_f = tril.astype(F32)
    s_prev = s_ref[...]
    for ci in range(tt // c):
        sl = slice(ci * c, (ci + 1) * c)
        g = g_all[sl]
        bc = jnp.dot(tril_f, g, preferred_element_type=F32, precision=lax.Precision.HIGHEST)
        b_last = bc[c - 1:c]
        q = gq_ref[0, sl, :].astype(F32) * (GLA_KDIM ** -0.5)
        k = gk_ref[0, sl, :].astype(F32)
        v = gv_ref[0, sl, :]
        q_t = (q * jnp.exp(bc)).astype(BF16)
        k_t = (k * jnp.exp(-bc)).astype(BF16)
        k_e = (k * jnp.exp(b_last - bc)).astype(BF16)
        att = lax.dot_general(q_t, k_t, (((1,), (1,)), ((), ())), preferred_element_type=F32)
        att = jnp.where(tril, att, 0.0).astype(BF16)
        o = (jnp.dot(att, v, preferred_element_type=F32)
             + jnp.dot(q_t, s_prev.astype(BF16), preferred_element_type=F32))
        kv = lax.dot_general(k_e, v, (((0,), (0,)), ((), ())), preferred_element_type=F32)
        s_prev = _row_to_col(jnp.exp(b_last)) * s_prev + kv
        ms = jnp.mean(o * o, axis=-1, keepdims=True)
        gate = go_ref[0, sl, :].astype(F32)
        gate = gate * jax.nn.sigmoid(gate)
        o_ref[0, sl, :] = (o * lax.rsqrt(ms + EPS) * ng_ref[...] * gate).astype(o_ref.dtype)
    s_ref[...] = s_prev

    @pl.when(ti == pl.num_programs(2) - 1)
    def _():
        sfin_ref[0, 0] = s_prev


def _gla_prompt(gall, glr, w2p, gate_b, norm_g):
    b, t, _ = gall.shape
    tt = _tile(t, 512)
    hk = GLA_HEADS
    return pl.pallas_call(
        functools.partial(_gla_kernel, tt=tt),
        out_shape=[
            jax.ShapeDtypeStruct((b, t, GLA_HEADS * GLA_VDIM), BF16),
            jax.ShapeDtypeStruct((b, GLA_HEADS, GLA_KDIM, GLA_VDIM), F32),
        ],
        grid=(b, GLA_HEADS, t // tt),
        in_specs=[
            pl.BlockSpec((1, tt, GLA_KDIM), lambda b, h, i: (b, i, h)),
            pl.BlockSpec((1, tt, GLA_KDIM), lambda b, h, i: (b, i, hk + h)),
            pl.BlockSpec((1, tt, GLA_VDIM), lambda b, h, i: (b, i, hk + h)),
            pl.BlockSpec((1, tt, GLA_VDIM), lambda b, h, i: (b, i, 2 * hk + h)),
            pl.BlockSpec((1, tt, LANES), lambda b, h, i: (b, i, 0)),
            pl.BlockSpec((LANES, GLA_KDIM), lambda b, h, i: (0, h)),
            pl.BlockSpec((1, GLA_KDIM), lambda b, h, i: (0, h)),
            pl.BlockSpec((1, GLA_VDIM), lambda b, h, i: (0, 0)),
        ],
        out_specs=[
            pl.BlockSpec((1, tt, GLA_VDIM), lambda b, h, i: (b, i, h)),
            pl.BlockSpec((1, 1, GLA_KDIM, GLA_VDIM), lambda b, h, i: (b, h, 0, 0)),
        ],
        scratch_shapes=[pltpu.VMEM((GLA_KDIM, GLA_VDIM), F32)],
        compiler_params=_params("parallel", "parallel", "arbitrary"),
        name="gla_prompt",
    )(gall, gall, gall, gall, glr, w2p, gate_b.reshape(1, -1), norm_g.reshape(1, GLA_VDIM))


def _outproj_kernel(ma_ref, mb_ref, wa_ref, wb_ref, x_ref, g1_ref, ng_ref, sc_ref, sh_ref,
                    x1_ref, h2_ref):
    y = (jnp.dot(ma_ref[0], wa_ref[...], preferred_element_type=F32)
         + jnp.dot(mb_ref[0], wb_ref[...], preferred_element_type=F32))
    x1 = x_ref[0] + g1_ref[0] * y
    x1_ref[0] = x1
    ms = jnp.mean(x1 * x1, axis=-1, keepdims=True)
    h = x1 * lax.rsqrt(ms + EPS) * ng_ref[...]
    h2_ref[0] = (h * (1.0 + sc_ref[0]) + sh_ref[0]).astype(h2_ref.dtype)


def _outproj(mix_a, mix_b, w_a, w_b, x, gate1, norm2_g, scale2, shift2):
    b, t, d = x.shape
    ka = mix_a.shape[2]
    kb = mix_b.shape[2]
    tt = _tile(t, 256)
    row = lambda b, t: (b, t, 0)
    const = lambda b, t: (0, 0)
    return pl.pallas_call(
        _outproj_kernel,
        out_shape=[jax.ShapeDtypeStruct((b, t, d), F32), jax.ShapeDtypeStruct((b, t, d), BF16)],
        grid=(b, t // tt),
        in_specs=[
            pl.BlockSpec((1, tt, ka), row),
            pl.BlockSpec((1, tt, kb), row),
            pl.BlockSpec((ka, d), const),
            pl.BlockSpec((kb, d), const),
            pl.BlockSpec((1, tt, d), row),
            _mod_spec(gate1, tt, d),
            pl.BlockSpec((1, d), const),
            _mod_spec(scale2, tt, d),
            _mod_spec(shift2, tt, d),
        ],
        out_specs=[pl.BlockSpec((1, tt, d), row), pl.BlockSpec((1, tt, d), row)],
        compiler_params=_params("parallel", "parallel"),
        name="outproj",
    )(mix_a, mix_b, w_a, w_b, x, gate1, norm2_g.reshape(1, d), scale2, shift2)


def _mlp_kernel(h_ref, wu_ref, wd_ref, x_ref, g2_ref, o_ref, acc_ref):
    f = pl.program_id(2)

    @pl.when(f == 0)
    def _():
        acc_ref[...] = jnp.zeros_like(acc_ref)

    u = jnp.dot(h_ref[0], wu_ref[...], preferred_element_type=F32)
    u = jnp.maximum(u, 0.0)
    u = (u * u).astype(BF16)
    acc_ref[...] += jnp.dot(u, wd_ref[...], preferred_element_type=F32)

    @pl.when(f == pl.num_programs(2) - 1)
    def _():
        o_ref[0] = x_ref[0] + g2_ref[0] * acc_ref[...]


def _mlp(h2, w_up, w_down, x1, gate2):
    b, t, d = x1.shape
    ff = w_up.shape[1]
    tt = _tile(t, 512)
    tf = _tile(ff, 1024)
    if gate2.shape[1] == 1:
        g_spec = pl.BlockSpec((1, 1, d), lambda b, t, f: (b, 0, 0))
    else:
        g_spec = pl.BlockSpec((1, tt, d), lambda b, t, f: (b, t, 0))
    row = lambda b, t, f: (b, t, 0)
    return pl.pallas_call(
        _mlp_kernel,
        out_shape=jax.ShapeDtypeStruct((b, t, d), F32),
        grid=(b, t // tt, ff // tf),
        in_specs=[
            pl.BlockSpec((1, tt, d), row),
            pl.BlockSpec((d, tf), lambda b, t, f: (0, f)),
            pl.BlockSpec((tf, d), lambda b, t, f: (f, 0)),
            pl.BlockSpec((1, tt, d), row),
            g_spec,
        ],
        out_specs=pl.BlockSpec((1, tt, d), row),
        scratch_shapes=[pltpu.VMEM((tt, d), F32)],
        compiler_params=_params("parallel", "parallel", "arbitrary"),
        name="mlp",
    )(h2, w_up, w_down, x1, gate2)


def _decode_kernel(*refs, lam_init, n_grp):
    pt_ref, q_ref, kn_ref, vn_ref = refs[:4]
    ck_refs = refs[4:4 + n_grp]
    cv_refs = refs[4 + n_grp:4 + 2 * n_grp]
    lq1, lk1, lq2, lk2, g_ref, o_ref, m_ref, l_ref, acc_ref = refs[4 + 2 * n_grp:]
    del pt_ref
    p = pl.program_id(1)
    nh = DA_HEADS
    page = ck_refs[0].shape[0]
    cols = page * nh

    @pl.when(p == 0)
    def _():
        m_ref[...] = jnp.full_like(m_ref, -jnp.inf)
        l_ref[...] = jnp.zeros_like(l_ref)
        acc_ref[...] = jnp.zeros_like(acc_ref)

    q2 = _split_halves(q_ref[0]).astype(BF16)
    row_h = lax.broadcasted_iota(jnp.int32, (2 * nh, cols), 0) % nh
    col_h = lax.broadcasted_iota(jnp.int32, (2 * nh, cols), 1) % nh
    valid = row_h == col_h
    scores = []
    for ck_ref in ck_refs:
        kp = ck_ref[...].reshape(cols, DA_VDIM).astype(BF16)
        s = lax.dot_general(q2, kp, (((1,), (1,)), ((), ())), preferred_element_type=F32)
        scores.append(jnp.where(valid, s, -jnp.inf))
    m_prev = m_ref[...]
    m_new = m_prev
    for s in scores:
        m_new = jnp.maximum(m_new, jnp.max(s, axis=1, keepdims=True))
    alpha = jnp.exp2(m_prev - m_new)
    l = alpha * l_ref[...]
    acc = acc_ref[...] * alpha
    for s, cv_ref in zip(scores, cv_refs):
        pr = jnp.exp2(s - m_new)
        l = l + jnp.sum(pr, axis=1, keepdims=True)
        vp = cv_ref[...].reshape(cols, DA_VDIM).astype(BF16)
        acc = acc + jnp.dot(pr.astype(BF16), vp, preferred_element_type=F32)
    l_ref[...] = l
    acc_ref[...] = acc
    m_ref[...] = m_new

    @pl.when(p == pl.num_programs(1) - 1)
    def _():
        kn = kn_ref[0]
        vn = vn_ref[0]
        k2 = jnp.concatenate([kn, kn], axis=0)
        v2 = jnp.concatenate([vn, vn], axis=0)
        s_new = jnp.sum(q2.astype(F32) * k2, axis=1, keepdims=True)
        m_fin = jnp.maximum(m_new, s_new)
        a = jnp.exp2(m_new - m_fin)
        pn = jnp.exp2(s_new - m_fin)
        o = (acc * a + pn * v2) / (a * l + pn)
        lam = _lambda_value(lq1, lk1, lq2, lk2, lam_init)
        od = o[:nh] - lam * o[nh:]
        ms = jnp.mean(od * od, axis=-1, keepdims=True)
        o_ref[0] = od * lax.rsqrt(ms + EPS) * g_ref[...] * (1.0 - lam_init)


def _attn_sample(q, k_new, v_new, cache_k, cache_v, page_table, lams, subln_g, lam_init):
    db, nh, dv = q.shape
    n_pages = page_table.shape[1]
    page = cache_k.shape[1]
    n_grp = max(g for g in range(1, DECODE_PAGES_PER_STEP + 1) if n_pages % g == 0)
    tok = pl.BlockSpec((1, nh, dv), lambda b, p, pt: (b, 0, 0))
    cache = [pl.BlockSpec((None, page, nh, dv), lambda b, p, pt, j=j: (pt[b, p * n_grp + j], 0, 0, 0))
             for j in range(n_grp)]
    vec = pl.BlockSpec((1, DA_QKDIM), lambda b, p, pt: (0, 0))
    grid_spec = pltpu.PrefetchScalarGridSpec(
        num_scalar_prefetch=1,
        grid=(db, n_pages // n_grp),
        in_specs=[tok, tok, tok, *cache, *cache, vec, vec, vec, vec,
                  pl.BlockSpec((1, dv), lambda b, p, pt: (0, 0))],
        out_specs=pl.BlockSpec((1, nh, dv), lambda b, p, pt: (b, 0, 0)),
        scratch_shapes=[
            pltpu.VMEM((2 * nh, 1), F32),
            pltpu.VMEM((2 * nh, 1), F32),
            pltpu.VMEM((2 * nh, dv), F32),
        ],
    )
    return pl.pallas_call(
        functools.partial(_decode_kernel, lam_init=lam_init, n_grp=n_grp),
        out_shape=jax.ShapeDtypeStruct((db, nh, dv), F32),
        grid_spec=grid_spec,
        compiler_params=_params("parallel", "arbitrary"),
        name="diff_attn_sample",
    )(page_table, q, k_new, v_new, *([cache_k] * n_grp), *([cache_v] * n_grp),
      *[x.reshape(1, DA_QKDIM) for x in lams], subln_g.reshape(1, dv))


def _gla_step_kernel(g_ref, glr_ref, w2_ref, b_ref, ng_ref, s0_ref, o_ref, s_ref):
    hk = GLA_HEADS * GLA_KDIM
    glr = jnp.broadcast_to(glr_ref[0], (8, LANES))
    g_all = _gate_log(glr, w2_ref[...], b_ref[...])[0:1]
    for h in range(GLA_HEADS):
        ks = slice(h * GLA_KDIM, (h + 1) * GLA_KDIM)
        q = g_ref[0, :, ks].astype(F32) * (GLA_KDIM ** -0.5)
        k = g_ref[0, :, hk + h * GLA_KDIM:hk + (h + 1) * GLA_KDIM].astype(F32)
        v = g_ref[0, :, 2 * hk + h * GLA_VDIM:2 * hk + (h + 1) * GLA_VDIM].astype(F32)
        gate = g_ref[0, :, 2 * hk + (GLA_HEADS + h) * GLA_VDIM:
                     2 * hk + (GLA_HEADS + h + 1) * GLA_VDIM].astype(F32)
        s = _row_to_col(jnp.exp(g_all[:, ks])) * s0_ref[h] + _row_to_col(k) * v
        s_ref[h] = s
        o = jnp.sum(_row_to_col(q) * s, axis=0, keepdims=True)
        ms = jnp.mean(o * o, axis=-1, keepdims=True)
        o_ref[0, :, h * GLA_VDIM:(h + 1) * GLA_VDIM] = (
            o * lax.rsqrt(ms + EPS) * ng_ref[...] * (gate * jax.nn.sigmoid(gate)))


def _gla_sample(gall, glr, w2p, gate_b, norm_g, state):
    db = gall.shape[0]
    wg = gall.shape[2]
    const = lambda b: (0, 0)
    return pl.pallas_call(
        _gla_step_kernel,
        out_shape=[
            jax.ShapeDtypeStruct((db, 1, GLA_HEADS * GLA_VDIM), F32),
            jax.ShapeDtypeStruct(state.shape, F32),
        ],
        grid=(db,),
        in_specs=[
            pl.BlockSpec((1, 1, wg), lambda b: (b, 0, 0)),
            pl.BlockSpec((1, 1, LANES), lambda b: (b, 0, 0)),
            pl.BlockSpec((LANES, GLA_HEADS * GLA_KDIM), const),
            pl.BlockSpec((1, GLA_HEADS * GLA_KDIM), const),
            pl.BlockSpec((1, GLA_VDIM), const),
            pl.BlockSpec((None, GLA_HEADS, GLA_KDIM, GLA_VDIM), lambda b: (b, 0, 0, 0)),
        ],
        out_specs=[
            pl.BlockSpec((1, 1, GLA_HEADS * GLA_VDIM), lambda b: (b, 0, 0)),
            pl.BlockSpec((None, GLA_HEADS, GLA_KDIM, GLA_VDIM), lambda b: (b, 0, 0, 0)),
        ],
        compiler_params=_params("parallel"),
        name="gla_sample",
    )(gall, glr, w2p, gate_b.reshape(1, -1), norm_g.reshape(1, GLA_VDIM), state)


def _layer_weights(w_in, gate_w2, w_out, w_up, w_down):
    da = DA_HEADS * DA_VDIM
    gk = GLA_HEADS * GLA_KDIM
    gv = GLA_HEADS * GLA_VDIM
    c = [0, da, 2 * da, 3 * da, 3 * da + gk, 3 * da + 2 * gk, 3 * da + 2 * gk + gv]
    c_glr = c[6]
    c_gout = c_glr + GATE_RANK
    wb = w_in.astype(BF16)
    w_gall = jnp.concatenate([wb[:, c[3]:c[6]], wb[:, c_gout:c_gout + gv]], axis=1)
    w_glr = jnp.pad(wb[:, c_glr:c_gout], ((0, 0), (0, LANES - GATE_RANK)))
    w2p = jnp.pad(gate_w2.astype(F32), ((0, LANES - GATE_RANK), (0, 0)))
    wo = w_out.astype(BF16)
    return dict(
        wq=wb[:, c[0]:c[1]], wk=wb[:, c[1]:c[2]], wv=wb[:, c[2]:c[3]],
        w_gall=w_gall, w_glr=w_glr, w2p=w2p,
        wo_a=wo[:da], wo_b=wo[da:], w_up=w_up.astype(BF16), w_down=w_down.astype(BF16))


def _mixer_inputs(x, scale1, shift1, norm1_g, lw, q_norm_g, k_norm_g):
    b, t, d = x.shape
    h = _prenorm(x, norm1_g, scale1, shift1).reshape(b * t, d)
    (q,) = _proj(h, lw["wq"], [BF16], norm_g=q_norm_g, out_scale=(Q_SCALE,))
    k32, k16 = _proj(h, lw["wk"], [F32, BF16], norm_g=k_norm_g)
    v32, v16 = _proj(h, lw["wv"], [F32, BF16])
    (gall,) = _proj(h, lw["w_gall"], [BF16])
    (glr,) = _proj(h, lw["w_glr"], [F32])
    return q, k32, k16, v32, v16, gall, glr


def kernel(x_prompt, x_sample, cache_k, cache_v, state_gla, page_table, c_prompt, c_sample, norm1_g, ada_w, ada_b, w_in, q_norm_g, k_norm_g, lambda_q1, lambda_k1, lambda_q2, lambda_k2, diff_subln_g, gla_gate_w2, gla_gate_b, gla_norm_g, w_out, norm2_g, w_up, w_down):
    depth = w_in.shape[0]
    bp, tp, d = x_prompt.shape
    db, ts, _ = x_sample.shape
    assert ts == 1, "sample group decodes one token per sequence"
    da = DA_HEADS * DA_VDIM
    xp = x_prompt
    xs = x_sample.reshape(1, db, d)
    outs = [[] for _ in range(6)]
    n_ctl = bp + db
    n_pad = -n_ctl % 16
    c_all = jnp.pad(jnp.concatenate([c_prompt, c_sample], axis=0), ((0, n_pad), (0, 0)))
    for i in range(depth):
        lam_init = 0.8 - 0.6 * math.exp(-0.3 * i)
        lams = (lambda_q1[i], lambda_k1[i], lambda_q2[i], lambda_k2[i])
        lw = _layer_weights(w_in[i], gla_gate_w2[i], w_out[i], w_up[i], w_down[i])
        mod = _adaln(c_all, ada_w[i], ada_b[i])
        mods_p = [m[:bp, None, :] for m in jnp.split(mod, 6, axis=-1)]
        mods_s = [m[None, bp:bp + db, :] for m in jnp.split(mod, 6, axis=-1)]

        sh1, sc1, g1, sh2, sc2, g2 = mods_p
        q, k32, k16, v32, v16, gall, glr = _mixer_inputs(xp, sc1, sh1, norm1_g[i], lw, q_norm_g[i], k_norm_g[i])
        o_da = _attn_prompt(q.reshape(bp, tp, da), k16.reshape(bp, tp, da), v16.reshape(bp, tp, da),
                            lams, diff_subln_g[i], lam_init)
        o_gla, s_p = _gla_prompt(gall.reshape(bp, tp, -1), glr.reshape(bp, tp, LANES), lw["w2p"],
                                 gla_gate_b[i], gla_norm_g[i])
        outs[0].append(k32.reshape(bp, tp, DA_HEADS, DA_VDIM))
        outs[1].append(v32.reshape(bp, tp, DA_HEADS, DA_VDIM))
        outs[2].append(s_p)
        x1, h2 = _outproj(o_da, o_gla, lw["wo_a"], lw["wo_b"], xp, g1, norm2_g[i], sc2, sh2)
        xp = _mlp(h2, lw["w_up"], lw["w_down"], x1, g2)

        sh1, sc1, g1, sh2, sc2, g2 = mods_s
        q, k32, k16, v32, v16, gall, glr = _mixer_inputs(xs, sc1, sh1, norm1_g[i], lw, q_norm_g[i], k_norm_g[i])
        o_da = _attn_sample(q.astype(F32).reshape(db, DA_HEADS, DA_VDIM),k32.reshape(db, DA_HEADS, DA_VDIM),
                            v32.reshape(db, DA_HEADS, DA_VDIM), cache_k[i], cache_v[i], page_table,
                            lams, diff_subln_g[i], lam_init)
        o_gla, s_s = _gla_sample(gall.reshape(db, 1, -1), glr.reshape(db, 1, LANES), lw["w2p"],
                                 gla_gate_b[i], gla_norm_g[i], state_gla[i])
        outs[3].append(k32.reshape(db, 1, DA_HEADS, DA_VDIM))
        outs[4].append(v32.reshape(db, 1, DA_HEADS, DA_VDIM))
        outs[5].append(s_s)
        x1, h2 = _outproj(o_da.reshape(1, db, da).astype(BF16), o_gla.reshape(1, db, -1).astype(BF16),
                          lw["wo_a"], lw["wo_b"], xs, g1, norm2_g[i], sc2, sh2)
        xs = _mlp(h2, lw["w_up"], lw["w_down"], x1, g2)

    return (xp, xs.reshape(db, 1, d), *[jnp.stack(o) for o in outs])
```

```python
import functools
import math

import jax
import jax.numpy as jnp
from jax import lax
from jax.experimental import pallas as pl
from jax.experimental.pallas import tpu as pltpu

F32 = jnp.float32
BF16 = jnp.bfloat16

EPS = 1e-6
DA_HEADS = 8
DA_VDIM = 128
DA_QKDIM = 64
GLA_HEADS = 4
GLA_KDIM = 128
GLA_VDIM = 256
GATE_RANK = 16
GATE_NORMALIZER = 16.0
GLA_CHUNK = 64
LANES = 128
Q_SCALE = DA_QKDIM ** -0.5 * math.log2(math.e)
DECODE_PAGES_PER_STEP = 8
ATTN_Q_BLOCK = 1024
ATTN_K_BLOCK = 512
ATTN_ROW_SUB = 256

VMEM_LIMIT = 56 * 1024 * 1024


def _params(*sem):
    return pltpu.CompilerParams(dimension_semantics=sem, vmem_limit_bytes=VMEM_LIMIT)


def _tile(n, t):
    t = min(n, t)
    assert n % t == 0, (n, t)
    return t


def _ada_kernel(c_ref, w_ref, b_ref, o_ref):
    c = c_ref[...]
    s = (c * jax.nn.sigmoid(c)).astype(BF16)
    o_ref[...] = jnp.dot(s, w_ref[...].astype(BF16), preferred_element_type=F32) + b_ref[...]


def _adaln(c, ada_w, ada_b):
    r, d = c.shape
    n = ada_w.shape[1]
    tn = _tile(n, 1024)
    return pl.pallas_call(
        _ada_kernel,
        out_shape=jax.ShapeDtypeStruct((r, n), F32),
        grid=(n // tn,),
        in_specs=[
            pl.BlockSpec((r, d), lambda j: (0, 0)),
            pl.BlockSpec((d, tn), lambda j: (0, j)),
            pl.BlockSpec((1, tn), lambda j: (0, j)),
        ],
        out_specs=pl.BlockSpec((r, tn), lambda j: (0, j)),
        compiler_params=_params("parallel"),
        name="adaln",
    )(c, ada_w, ada_b.reshape(1, n))


def _prenorm_kernel(x_ref, g_ref, sc_ref, sh_ref, o_ref):
    x = x_ref[0]
    ms = jnp.mean(x * x, axis=-1, keepdims=True)
    y = x * lax.rsqrt(ms + EPS) * g_ref[...]
    o_ref[0] = (y * (1.0 + sc_ref[0]) + sh_ref[0]).astype(o_ref.dtype)


def _mod_spec(mod, tt, d):
    if mod.shape[1] == 1:
        return pl.BlockSpec((1, 1, d), lambda b, t: (b, 0, 0))
    return pl.BlockSpec((1, tt, d), lambda b, t: (b, t, 0))


def _prenorm(x, g, scale, shift):
    b, t, d = x.shape
    tt = _tile(t, 512)
    return pl.pallas_call(
        _prenorm_kernel,
        out_shape=jax.ShapeDtypeStruct((b, t, d), BF16),
        grid=(b, t // tt),
        in_specs=[
            pl.BlockSpec((1, tt, d), lambda b, t: (b, t, 0)),
            pl.BlockSpec((1, d), lambda b, t: (0, 0)),
            _mod_spec(scale, tt, d),
            _mod_spec(shift, tt, d),
        ],
        out_specs=pl.BlockSpec((1, tt, d), lambda b, t: (b, t, 0)),
        compiler_params=_params("parallel", "parallel"),
        name="prenorm",
    )(x, g.reshape(1, d), scale, shift)


def _proj_kernel(*refs, norm, out_scale, n_out):
    if norm:
        h_ref, w_ref, g_ref, gm_ref = refs[:4]
        outs = refs[4:4 + n_out]
    else:
        h_ref, w_ref = refs[:2]
        outs = refs[2:2 + n_out]
    acc = jnp.dot(h_ref[...], w_ref[...], preferred_element_type=F32)
    if not norm:
        for o in outs:
            o[...] = acc.astype(o.dtype)
        return
    tn = acc.shape[1]
    for cb in range(tn // LANES):
        y = acc[:, cb * LANES:(cb + 1) * LANES]
        ss = jnp.dot((y * y).astype(BF16), gm_ref[...], preferred_element_type=F32)
        yn = y * lax.rsqrt(ss * (1.0 / DA_QKDIM) + EPS) * g_ref[...]
        for o, s in zip(outs, out_scale):
            o[:, cb * LANES:(cb + 1) * LANES] = (yn * s if s != 1.0 else yn).astype(o.dtype)


def _proj(h, w, out_dtypes, *, norm_g=None, out_scale=None):
    m, k = h.shape
    n = w.shape[1]
    tm = _tile(m, 1024)
    tn = _tile(n, 1024)
    norm = norm_g is not None
    n_out = len(out_dtypes)
    if out_scale is None:
        out_scale = (1.0,) * n_out
    in_specs = [
        pl.BlockSpec((tm, k), lambda j, i: (i, 0)),
        pl.BlockSpec((k, tn), lambda j, i: (0, j)),
    ]
    args = [h, w]
    if norm:
        g2 = jnp.concatenate([norm_g, norm_g]).reshape(1, LANES).astype(F32)
        lane = jnp.arange(LANES) // DA_QKDIM
        gm = (lane[:, None] == lane[None, :]).astype(BF16)
        in_specs += [
            pl.BlockSpec((1, LANES), lambda j, i: (0, 0)),
            pl.BlockSpec((LANES, LANES), lambda j, i: (0, 0)),
        ]
        args += [g2, gm]
    outs = pl.pallas_call(
        functools.partial(_proj_kernel, norm=norm, out_scale=tuple(out_scale), n_out=n_out),
        out_shape=[jax.ShapeDtypeStruct((m, n), dt) for dt in out_dtypes],
        grid=(n // tn, m // tm),
        in_specs=in_specs,
        out_specs=[pl.BlockSpec((tm, tn), lambda j, i: (i, j)) for _ in out_dtypes],
        compiler_params=_params("parallel", "parallel"),
        name="proj_norm" if norm else "proj",
    )(*args)
    return outs


def _lambda_value(lq1, lk1, lq2, lk2, lam_init):
    a = jnp.sum(lq1[...] * lk1[...], axis=-1, keepdims=True)
    b = jnp.sum(lq2[...] * lk2[...], axis=-1, keepdims=True)
    return jnp.exp(a) - jnp.exp(b) + lam_init


def _split_halves(q):
    lane = lax.broadcasted_iota(jnp.int32, q.shape, 1)
    zero = jnp.zeros_like(q)
    return jnp.concatenate(
        [jnp.where(lane < DA_QKDIM, q, zero), jnp.where(lane >= DA_QKDIM, q, zero)], axis=0)


def _attn_kernel(q_ref, k_ref, v_ref, lq1, lk1, lq2, lk2, g_ref, o_ref,
                 vx_ref, q2_ref, m_ref, acc_ref, sa_ref, sb_ref, *, bq, bk, sub, lam_init):
    qi = pl.program_id(2)
    rows = 2 * bq
    ratio = bq // bk

    @pl.when(qi == 0)
    def _():
        vx_ref[:, :DA_VDIM] = v_ref[0]
        vx_ref[:, DA_VDIM:] = jnp.ones((vx_ref.shape[0], DA_VDIM), vx_ref.dtype)

    q2_ref[...] = _split_halves(q_ref[0])
    m_ref[...] = jnp.full_like(m_ref, -jnp.inf)
    acc_ref[...] = jnp.zeros_like(acc_ref)

    def n_cols(q_lo, diag):
        return bk if diag is None else max(0, min(bk, q_lo + sub - diag * bk))

    def scores(ki, s_ref, diag=None):
        start = pl.multiple_of(ki * bk, bk)
        for r in range(rows // sub):
            rs = slice(r * sub, (r + 1) * sub)
            ncol = n_cols((r * sub) % bq, diag)
            if ncol:
                k = k_ref[0, pl.ds(start, ncol), :]
                s_ref[rs, :ncol] = lax.dot_general(q2_ref[rs], k, (((1,), (1,)), ((), ())),
                                                   preferred_element_type=F32)

    def update(ki, s_ref, diag=None):
        start = pl.multiple_of(ki * bk, bk)
        for r in range(rows // sub):
            rs = slice(r * sub, (r + 1) * sub)
            q_lo = (r * sub) % bq
            ncol = n_cols(q_lo, diag)
            if not ncol:
                continue
            vx = vx_ref[pl.ds(start, ncol), :]
            s = s_ref[rs, :ncol]
            if diag is not None and diag * bk + ncol - 1 > q_lo:
                row = q_lo + lax.broadcasted_iota(jnp.int32, (sub, ncol), 0)
                col = diag * bk + lax.broadcasted_iota(jnp.int32, (sub, ncol), 1)
                s = jnp.where(col <= row, s, -jnp.inf)
            m_prev = m_ref[rs]
            m_new = jnp.maximum(m_prev, jnp.max(s, axis=1, keepdims=True))
            alpha = jnp.exp2(m_prev - m_new)
            p = jnp.exp2(s - jnp.tile(m_new, (1, ncol // LANES))).astype(BF16)
            acc_ref[rs] = (acc_ref[rs] * jnp.tile(alpha, (1, 2))
                           + jnp.dot(p, vx, preferred_element_type=F32))
            m_ref[rs] = m_new

    scores(0, sa_ref)

    def pair(j, carry):
        scores(2 * j + 1, sb_ref)
        update(2 * j, sa_ref)
        scores(2 * j + 2, sa_ref)
        update(2 * j + 1, sb_ref)
        return carry

    lax.fori_loop(0, qi * (ratio // 2), pair, 0)

    kd = qi * ratio
    bufs = (sa_ref, sb_ref)
    for d in range(ratio):
        if d + 1 < ratio:
            scores(kd + d + 1, bufs[(d + 1) % 2], diag=d + 1)
        update(kd + d, bufs[d % 2], diag=d)

    acc = acc_ref[...]
    o = acc[:, :DA_VDIM] / acc[:, DA_VDIM:]
    lam = _lambda_value(lq1, lk1, lq2, lk2, lam_init)
    od = o[:bq] - lam * o[bq:]
    ms = jnp.mean(od * od, axis=-1, keepdims=True)
    o_ref[0] = (od * lax.rsqrt(ms + EPS) * g_ref[...] * (1.0 - lam_init)).astype(o_ref.dtype)


def _attn_prompt(q, k, v, lams, subln_g, lam_init):
    b, t, w = q.shape
    h = w // DA_VDIM
    bq = _tile(t, ATTN_Q_BLOCK)
    bk = _tile(bq, ATTN_K_BLOCK)
    assert (bq // bk) % 2 == 0, "the two-buffer score pipeline consumes key blocks in pairs"
    sub = _tile(bk, ATTN_ROW_SUB)
    vec = pl.BlockSpec((1, DA_QKDIM), lambda b, h, i: (0, 0))
    return pl.pallas_call(
        functools.partial(_attn_kernel, bq=bq, bk=bk, sub=sub, lam_init=lam_init),
        out_shape=jax.ShapeDtypeStruct((b, t, w), BF16),
        grid=(b, h, t // bq),
        in_specs=[
            pl.BlockSpec((1, bq, DA_VDIM), lambda b, h, i: (b, i, h)),
            pl.BlockSpec((1, t, DA_VDIM), lambda b, h, i: (b, 0, h)),
            pl.BlockSpec((1, t, DA_VDIM), lambda b, h, i: (b, 0, h)),
            vec, vec, vec, vec,
            pl.BlockSpec((1, DA_VDIM), lambda b, h, i: (0, 0)),
        ],
        out_specs=pl.BlockSpec((1, bq, DA_VDIM), lambda b, h, i: (b, i, h)),
        scratch_shapes=[
            pltpu.VMEM((t, 2 * DA_VDIM), BF16),
            pltpu.VMEM((2 * bq, DA_VDIM), BF16),
            pltpu.VMEM((2 * bq, LANES), F32),
            pltpu.VMEM((2 * bq, 2 * DA_VDIM), F32),
            pltpu.VMEM((2 * bq, bk), F32),
            pltpu.VMEM((2 * bq, bk), F32),
        ],
        compiler_params=_params("parallel", "parallel", "arbitrary"),
        name="diff_attn_prompt",
    )(q, k, v, *[x.reshape(1, DA_QKDIM) for x in lams], subln_g.reshape(1, DA_VDIM))


def _log_sigmoid(z):
    return jnp.minimum(z, 0.0) - jnp.log1p(jnp.exp(-jnp.abs(z)))


def _row_to_col(row):
    n = row.shape[1]
    r = lax.broadcasted_iota(jnp.int32, (n, n), 0)
    c = lax.broadcasted_iota(jnp.int32, (n, n), 1)
    return jnp.sum(jnp.where(r == c, jnp.broadcast_to(row, (n, n)), 0.0), axis=1, keepdims=True)


def _split2(x):
    hi = x.astype(BF16)
    return hi, (x - hi.astype(F32)).astype(BF16)


def _dot_split(a, b):
    ah, al = _split2(a)
    bh, bl = _split2(b)
    return (jnp.dot(ah, bh, preferred_element_type=F32) + jnp.dot(ah, bl, preferred_element_type=F32)
            + jnp.dot(al, bh, preferred_element_type=F32))


def _gate_log(glr, w2, b):
    return _log_sigmoid(_dot_split(glr, w2) + b) * (1.0 / GATE_NORMALIZER)


def _gla_kernel(gq_ref, gkt_ref, gv_ref, go_ref, glr_ref, glrt_ref, w2_ref, w2t_ref, b_ref, bt_ref,
                ng_ref, o_ref, sfin_ref, s_ref, *, tt):
    ti = pl.program_id(2)

    @pl.when(ti == 0)
    def _():
        s_ref[...] = jnp.zeros_like(s_ref)

    c = GLA_CHUNK
    p2 = 2 * c
    g_all = _gate_log(glr_ref[0], w2_ref[...], b_ref[...])
    gt_all = _gate_log(w2t_ref[...], glrt_ref[0], bt_ref[...])
    r = lax.broadcasted_iota(jnp.int32, (p2, p2), 0)
    cc = lax.broadcasted_iota(jnp.int32, (p2, p2), 1)
    same = (r >= c) == (cc >= c)
    tril2 = same & (r >= cc)
    lane_lo = cc < c
    tril_b = tril2.astype(BF16)
    rhs_t = jnp.concatenate([same & (r <= cc), r < c, r >= c], axis=1).astype(BF16)
    s_prev = s_ref[...]
    for j in range(tt // p2):
        sl = slice(j * p2, (j + 1) * p2)
        gh, gl = _split2(g_all[sl])
        bc = (jnp.dot(tril_b, gh, preferred_element_type=F32)
              + jnp.dot(tril_b, gl, preferred_element_type=F32))
        th, tl = _split2(gt_all[:, sl])
        pre = (jnp.dot(th, rhs_t, preferred_element_type=F32)
               + jnp.dot(tl, rhs_t, preferred_element_type=F32))
        bc_t = pre[:, :p2]
        tot0 = pre[:, p2:2 * p2]
        tot1 = pre[:, 2 * p2:]
        q_t = (gq_ref[0, sl, :].astype(F32) * (GLA_KDIM ** -0.5) * jnp.exp(bc)).astype(BF16)
        k_all = gkt_ref[0, :, sl].astype(F32)
        k_t = (k_all * jnp.exp(-bc_t)).astype(BF16)
        k_e = k_all * jnp.exp(jnp.where(lane_lo, tot0, tot1) - bc_t)
        k_e0 = jnp.where(lane_lo, k_e, 0.0).astype(BF16)
        k_e1 = jnp.where(lane_lo, 0.0, k_e).astype(BF16)
        v = gv_ref[0, sl, :]
        att = jnp.dot(q_t, k_t, preferred_element_type=F32)
        att = jnp.where(tril2, att, 0.0).astype(BF16)
        o_intra = jnp.dot(att, v, preferred_element_type=F32)
        s_mid = jnp.tile(jnp.exp(tot0), (1, 2)) * s_prev + jnp.dot(k_e0, v, preferred_element_type=F32)
        s_next = jnp.tile(jnp.exp(tot1), (1, 2)) * s_mid + jnp.dot(k_e1, v, preferred_element_type=F32)
        o = jnp.concatenate([
            o_intra[:c] + jnp.dot(q_t[:c], s_prev.astype(BF16), preferred_element_type=F32),
            o_intra[c:] + jnp.dot(q_t[c:], s_mid.astype(BF16), preferred_element_type=F32),
        ], axis=0)
        s_prev = s_next
        ms = jnp.mean(o * o, axis=-1, keepdims=True)
        gate = go_ref[0, sl, :].astype(F32)
        gate = gate * jax.nn.sigmoid(gate)
        o_ref[0, sl, :] = (o * lax.rsqrt(ms + EPS) * ng_ref[...] * gate).astype(o_ref.dtype)
    s_ref[...] = s_prev

    @pl.when(ti == pl.num_programs(2) - 1)
    def _():
        sfin_ref[0, 0] = s_prev


def _gla_prompt(gall, glr, w2p, gate_w2, gate_b, norm_g):
    b, t, _ = gall.shape
    tt = _tile(t, 512)
    assert tt % (2 * GLA_CHUNK) == 0
    hk = GLA_HEADS
    kw = GLA_HEADS * GLA_KDIM
    gk_t = jnp.swapaxes(gall[:, :, kw:2 * kw], 1, 2)
    glr_t = jnp.swapaxes(glr[:, :, :GATE_RANK], 1, 2)
    return pl.pallas_call(
        functools.partial(_gla_kernel, tt=tt),
        out_shape=[
            jax.ShapeDtypeStruct((b, t, GLA_HEADS * GLA_VDIM), BF16),
            jax.ShapeDtypeStruct((b, GLA_HEADS, GLA_KDIM, GLA_VDIM), F32),
        ],
        grid=(b, GLA_HEADS, t // tt),
        in_specs=[
            pl.BlockSpec((1, tt, GLA_KDIM), lambda b, h, i: (b, i, h)),
            pl.BlockSpec((1, GLA_KDIM, tt), lambda b, h, i: (b, h, i)),
            pl.BlockSpec((1, tt, GLA_VDIM), lambda b, h, i: (b, i, hk + h)),
            pl.BlockSpec((1, tt, GLA_VDIM), lambda b, h, i: (b, i, 2 * hk + h)),
            pl.BlockSpec((1, tt, LANES), lambda b, h, i: (b, i, 0)),
            pl.BlockSpec((1, GATE_RANK, tt), lambda b, h, i: (b, 0, i)),
            pl.BlockSpec((LANES, GLA_KDIM), lambda b, h, i: (0, h)),
            pl.BlockSpec((GLA_KDIM, GATE_RANK), lambda b, h, i: (h, 0)),
            pl.BlockSpec((1, GLA_KDIM), lambda b, h, i: (0, h)),
            pl.BlockSpec((GLA_KDIM, 1), lambda b, h, i: (h, 0)),
            pl.BlockSpec((1, GLA_VDIM), lambda b, h, i: (0, 0)),
        ],
        out_specs=[
            pl.BlockSpec((1, tt, GLA_VDIM), lambda b, h, i: (b, i, h)),
            pl.BlockSpec((1, 1, GLA_KDIM, GLA_VDIM), lambda b, h, i: (b, h, 0, 0)),
        ],
        scratch_shapes=[pltpu.VMEM((GLA_KDIM, GLA_VDIM), F32)],
        compiler_params=_params("parallel", "parallel", "arbitrary"),
        name="gla_prompt",
    )(gall, gk_t, gall, gall, glr, glr_t, w2p, gate_w2.astype(F32).T, gate_b.reshape(1, -1),
      gate_b.reshape(-1, 1), norm_g.reshape(1, GLA_VDIM))


def _outproj_kernel(ma_ref, mb_ref, wa_ref, wb_ref, x_ref, g1_ref, ng_ref, sc_ref, sh_ref,
                    x1_ref, h2_ref):
    y = (jnp.dot(ma_ref[0], wa_ref[...], preferred_element_type=F32)
         + jnp.dot(mb_ref[0], wb_ref[...], preferred_element_type=F32))
    x1 = x_ref[0] + g1_ref[0] * y
    x1_ref[0] = x1
    ms = jnp.mean(x1 * x1, axis=-1, keepdims=True)
    h = x1 * lax.rsqrt(ms + EPS) * ng_ref[...]
    h2_ref[0] = (h * (1.0 + sc_ref[0]) + sh_ref[0]).astype(h2_ref.dtype)


def _outproj(mix_a, mix_b, w_a, w_b, x, gate1, norm2_g, scale2, shift2):
    b, t, d = x.shape
    ka = mix_a.shape[2]
    kb = mix_b.shape[2]
    tt = _tile(t, 256)
    row = lambda b, t: (b, t, 0)
    const = lambda b, t: (0, 0)
    return pl.pallas_call(
        _outproj_kernel,
        out_shape=[jax.ShapeDtypeStruct((b, t, d), F32), jax.ShapeDtypeStruct((b, t, d), BF16)],
        grid=(b, t // tt),
        in_specs=[
            pl.BlockSpec((1, tt, ka), row),
            pl.BlockSpec((1, tt, kb), row),
            pl.BlockSpec((ka, d), const),
            pl.BlockSpec((kb, d), const),
            pl.BlockSpec((1, tt, d), row),
            _mod_spec(gate1, tt, d),
            pl.BlockSpec((1, d), const),
            _mod_spec(scale2, tt, d),
            _mod_spec(shift2, tt, d),
        ],
        out_specs=[pl.BlockSpec((1, tt, d), row), pl.BlockSpec((1, tt, d), row)],
        compiler_params=_params("parallel", "parallel"),
        name="outproj",
    )(mix_a, mix_b, w_a, w_b, x, gate1, norm2_g.reshape(1, d), scale2, shift2)


def _mlp_kernel(h_ref, wu_ref, wd_ref, x_ref, g2_ref, o_ref, acc_ref):
    f = pl.program_id(2)

    @pl.when(f == 0)
    def _():
        acc_ref[...] = jnp.zeros_like(acc_ref)

    u = jnp.dot(h_ref[0], wu_ref[...], preferred_element_type=F32)
    u = jnp.maximum(u, 0.0)
    u = (u * u).astype(BF16)
    acc_ref[...] += jnp.dot(u, wd_ref[...], preferred_element_type=F32)

    @pl.when(f == pl.num_programs(2) - 1)
    def _():
        o_ref[0] = x_ref[0] + g2_ref[0] * acc_ref[...]


def _mlp(h2, w_up, w_down, x1, gate2):
    b, t, d = x1.shape
    ff = w_up.shape[1]
    tt = _tile(t, 512)
    tf = _tile(ff, 1024)
    if gate2.shape[1] == 1:
        g_spec = pl.BlockSpec((1, 1, d), lambda b, t, f: (b, 0, 0))
    else:
        g_spec = pl.BlockSpec((1, tt, d), lambda b, t, f: (b, t, 0))
    row = lambda b, t, f: (b, t, 0)
    return pl.pallas_call(
        _mlp_kernel,
        out_shape=jax.ShapeDtypeStruct((b, t, d), F32),
        grid=(b, t // tt, ff // tf),
        in_specs=[
            pl.BlockSpec((1, tt, d), row),
            pl.BlockSpec((d, tf), lambda b, t, f: (0, f)),
            pl.BlockSpec((tf, d), lambda b, t, f: (f, 0)),
            pl.BlockSpec((1, tt, d), row),
            g_spec,
        ],
        out_specs=pl.BlockSpec((1, tt, d), row),
        scratch_shapes=[pltpu.VMEM((tt, d), F32)],
        compiler_params=_params("parallel", "parallel", "arbitrary"),
        name="mlp",
    )(h2, w_up, w_down, x1, gate2)


def _decode_kernel(*refs, lam_init, n_grp):
    pt_ref, q_ref, kn_ref, vn_ref = refs[:4]
    ck_refs = refs[4:4 + n_grp]
    cv_refs = refs[4 + n_grp:4 + 2 * n_grp]
    lq1, lk1, lq2, lk2, g_ref, o_ref, m_ref, l_ref, acc_ref = refs[4 + 2 * n_grp:]
    del pt_ref
    p = pl.program_id(1)
    nh = DA_HEADS
    page = ck_refs[0].shape[0]
    cols = page * nh

    @pl.when(p == 0)
    def _():
        m_ref[...] = jnp.full_like(m_ref, -jnp.inf)
        l_ref[...] = jnp.zeros_like(l_ref)
        acc_ref[...] = jnp.zeros_like(acc_ref)

    q2 = _split_halves(q_ref[0]).astype(BF16)
    row_h = lax.broadcasted_iota(jnp.int32, (2 * nh, cols), 0) % nh
    col_h = lax.broadcasted_iota(jnp.int32, (2 * nh, cols), 1) % nh
    valid = row_h == col_h
    scores = []
    for ck_ref in ck_refs:
        kp = ck_ref[...].reshape(cols, DA_VDIM).astype(BF16)
        s = lax.dot_general(q2, kp, (((1,), (1,)), ((), ())), preferred_element_type=F32)
        scores.append(jnp.where(valid, s, -jnp.inf))
    m_prev = m_ref[...]
    m_new = m_prev
    for s in scores:
        m_new = jnp.maximum(m_new, jnp.max(s, axis=1, keepdims=True))
    alpha = jnp.exp2(m_prev - m_new)
    l = alpha * l_ref[...]
    acc = acc_ref[...] * alpha
    for s, cv_ref in zip(scores, cv_refs):
        pr = jnp.exp2(s - m_new)
        l = l + jnp.sum(pr, axis=1, keepdims=True)
        vp = cv_ref[...].reshape(cols, DA_VDIM).astype(BF16)
        acc = acc + jnp.dot(pr.astype(BF16), vp, preferred_element_type=F32)
    l_ref[...] = l
    acc_ref[...] = acc
    m_ref[...] = m_new

    @pl.when(p == pl.num_programs(1) - 1)
    def _():
        kn = kn_ref[0]
        vn = vn_ref[0]
        k2 = jnp.concatenate([kn, kn], axis=0)
        v2 = jnp.concatenate([vn, vn], axis=0)
        s_new = jnp.sum(q2.astype(F32) * k2, axis=1, keepdims=True)
        m_fin = jnp.maximum(m_new, s_new)
        a = jnp.exp2(m_new - m_fin)
        pn = jnp.exp2(s_new - m_fin)
        o = (acc * a + pn * v2) / (a * l + pn)
        lam = _lambda_value(lq1, lk1, lq2, lk2, lam_init)
        od = o[:nh] - lam * o[nh:]
        ms = jnp.mean(od * od, axis=-1, keepdims=True)
        o_ref[0] = od * lax.rsqrt(ms + EPS) * g_ref[...] * (1.0 - lam_init)


def _attn_sample(q, k_new, v_new, cache_k, cache_v, page_table, lams, subln_g, lam_init):
    db, nh, dv = q.shape
    n_pages = page_table.shape[1]
    page = cache_k.shape[1]
    n_grp = max(g for g in range(1, DECODE_PAGES_PER_STEP + 1) if n_pages % g == 0)
    tok = pl.BlockSpec((1, nh, dv), lambda b, p, pt: (b, 0, 0))
    cache = [pl.BlockSpec((None, page, nh, dv), lambda b, p, pt, j=j: (pt[b, p * n_grp + j], 0, 0, 0))
             for j in range(n_grp)]
    vec = pl.BlockSpec((1, DA_QKDIM), lambda b, p, pt: (0, 0))
    grid_spec = pltpu.PrefetchScalarGridSpec(
        num_scalar_prefetch=1,
        grid=(db, n_pages // n_grp),
        in_specs=[tok, tok, tok, *cache, *cache, vec, vec, vec, vec,
                  pl.BlockSpec((1, dv), lambda b, p, pt: (0, 0))],
        out_specs=pl.BlockSpec((1, nh, dv), lambda b, p, pt: (b, 0, 0)),
        scratch_shapes=[
            pltpu.VMEM((2 * nh, 1), F32),
            pltpu.VMEM((2 * nh, 1), F32),
            pltpu.VMEM((2 * nh, dv), F32),
        ],
    )
    return pl.pallas_call(
        functools.partial(_decode_kernel, lam_init=lam_init, n_grp=n_grp),
        out_shape=jax.ShapeDtypeStruct((db, nh, dv), F32),
        grid_spec=grid_spec,
        compiler_params=_params("parallel", "arbitrary"),
        name="diff_attn_sample",
    )(page_table, q, k_new, v_new, *([cache_k] * n_grp), *([cache_v] * n_grp),
      *[x.reshape(1, DA_QKDIM) for x in lams], subln_g.reshape(1, dv))


def _gla_step_kernel(g_ref, glr_ref, w2_ref, b_ref, ng_ref, s0_ref, o_ref, s_ref):
    hk = GLA_HEADS * GLA_KDIM
    glr = jnp.broadcast_to(glr_ref[0], (8, LANES))
    g_all = _gate_log(glr, w2_ref[...], b_ref[...])[0:1]
    for h in range(GLA_HEADS):
        ks = slice(h * GLA_KDIM, (h + 1) * GLA_KDIM)
        q = g_ref[0, :, ks].astype(F32) * (GLA_KDIM ** -0.5)
        k = g_ref[0, :, hk + h * GLA_KDIM:hk + (h + 1) * GLA_KDIM].astype(F32)
        v = g_ref[0, :, 2 * hk + h * GLA_VDIM:2 * hk + (h + 1) * GLA_VDIM].astype(F32)
        gate = g_ref[0, :, 2 * hk + (GLA_HEADS + h) * GLA_VDIM:
                     2 * hk + (GLA_HEADS + h + 1) * GLA_VDIM].astype(F32)
        s = _row_to_col(jnp.exp(g_all[:, ks])) * s0_ref[h] + _row_to_col(k) * v
        s_ref[h] = s
        o = jnp.sum(_row_to_col(q) * s, axis=0, keepdims=True)
        ms = jnp.mean(o * o, axis=-1, keepdims=True)
        o_ref[0, :, h * GLA_VDIM:(h + 1) * GLA_VDIM] = (
            o * lax.rsqrt(ms + EPS) * ng_ref[...] * (gate * jax.nn.sigmoid(gate)))


def _gla_sample(gall, glr, w2p, gate_b, norm_g, state):
    db = gall.shape[0]
    wg = gall.shape[2]
    const = lambda b: (0, 0)
    return pl.pallas_call(
        _gla_step_kernel,
        out_shape=[
            jax.ShapeDtypeStruct((db, 1, GLA_HEADS * GLA_VDIM), F32),
            jax.ShapeDtypeStruct(state.shape, F32),
        ],
        grid=(db,),
        in_specs=[
            pl.BlockSpec((1, 1, wg), lambda b: (b, 0, 0)),
            pl.BlockSpec((1, 1, LANES), lambda b: (b, 0, 0)),
            pl.BlockSpec((LANES, GLA_HEADS * GLA_KDIM), const),
            pl.BlockSpec((1, GLA_HEADS * GLA_KDIM), const),
            pl.BlockSpec((1, GLA_VDIM), const),
            pl.BlockSpec((None, GLA_HEADS, GLA_KDIM, GLA_VDIM), lambda b: (b, 0, 0, 0)),
        ],
        out_specs=[
            pl.BlockSpec((1, 1, GLA_HEADS * GLA_VDIM), lambda b: (b, 0, 0)),
            pl.BlockSpec((None, GLA_HEADS, GLA_KDIM, GLA_VDIM), lambda b: (b, 0, 0, 0)),
        ],
        compiler_params=_params("parallel"),
        name="gla_sample",
    )(gall, glr, w2p, gate_b.reshape(1, -1), norm_g.reshape(1, GLA_VDIM), state)


def _layer_weights(w_in, gate_w2, w_out, w_up, w_down):
    da = DA_HEADS * DA_VDIM
    gk = GLA_HEADS * GLA_KDIM
    gv = GLA_HEADS * GLA_VDIM
    c = [0, da, 2 * da, 3 * da, 3 * da + gk, 3 * da + 2 * gk, 3 * da + 2 * gk + gv]
    c_glr = c[6]
    c_gout = c_glr + GATE_RANK
    wb = w_in.astype(BF16)
    w_gall = jnp.concatenate([wb[:, c[3]:c[6]], wb[:, c_gout:c_gout + gv]], axis=1)
    w_glr = jnp.pad(wb[:, c_glr:c_gout], ((0, 0), (0, LANES - GATE_RANK)))
    w2p = jnp.pad(gate_w2.astype(F32), ((0, LANES - GATE_RANK), (0, 0)))
    wo = w_out.astype(BF16)
    return dict(
        wq=wb[:, c[0]:c[1]], wk=wb[:, c[1]:c[2]], wv=wb[:, c[2]:c[3]],
        w_gall=w_gall, w_glr=w_glr, w2p=w2p,
        wo_a=wo[:da], wo_b=wo[da:], w_up=w_up.astype(BF16), w_down=w_down.astype(BF16))


def _mixer_inputs(x, scale1, shift1, norm1_g, lw, q_norm_g, k_norm_g):
    b, t, d = x.shape
    h = _prenorm(x, norm1_g, scale1, shift1).reshape(b * t, d)
    (q,) = _proj(h, lw["wq"], [BF16], norm_g=q_norm_g, out_scale=(Q_SCALE,))
    k32, k16 = _proj(h, lw["wk"], [F32, BF16], norm_g=k_norm_g)
    v32, v16 = _proj(h, lw["wv"], [F32, BF16])
    (gall,) = _proj(h, lw["w_gall"], [BF16])
    (glr,) = _proj(h, lw["w_glr"], [F32])
    return q, k32, k16, v32, v16, gall, glr


def kernel(x_prompt, x_sample, cache_k, cache_v, state_gla, page_table, c_prompt, c_sample, norm1_g, ada_w, ada_b, w_in, q_norm_g, k_norm_g, lambda_q1, lambda_k1, lambda_q2, lambda_k2, diff_subln_g, gla_gate_w2, gla_gate_b, gla_norm_g, w_out, norm2_g, w_up, w_down):
    depth = w_in.shape[0]
    bp, tp, d = x_prompt.shape
    db, ts, _ = x_sample.shape
    assert ts == 1, "sample group decodes one token per sequence"
    da = DA_HEADS * DA_VDIM
    xp = x_prompt
    xs = x_sample.reshape(1, db, d)
    outs = [[] for _ in range(6)]
    n_ctl = bp + db
    n_pad = -n_ctl % 16
    c_all = jnp.pad(jnp.concatenate([c_prompt, c_sample], axis=0), ((0, n_pad), (0, 0)))
    for i in range(depth):
        lam_init = 0.8 - 0.6 * math.exp(-0.3 * i)
        lams = (lambda_q1[i], lambda_k1[i], lambda_q2[i], lambda_k2[i])
        lw = _layer_weights(w_in[i], gla_gate_w2[i], w_out[i], w_up[i], w_down[i])
        mod = _adaln(c_all, ada_w[i], ada_b[i])
        mods_p = [m[:bp, None, :] for m in jnp.split(mod, 6, axis=-1)]
        mods_s = [m[None, bp:bp + db, :] for m in jnp.split(mod, 6, axis=-1)]

        sh1, sc1, g1, sh2, sc2, g2 = mods_p
        q, k32, k16, v32, v16, gall, glr = _mixer_inputs(xp, sc1, sh1, norm1_g[i], lw, q_norm_g[i], k_norm_g[i])
        o_da = _attn_prompt(q.reshape(bp, tp, da), k16.reshape(bp, tp, da), v16.reshape(bp, tp, da),
                            lams, diff_subln_g[i], lam_init)
        o_gla, s_p = _gla_prompt(gall.reshape(bp, tp, -1), glr.reshape(bp, tp, LANES), lw["w2p"],
                                 gla_gate_w2[i], gla_gate_b[i], gla_norm_g[i])
        outs[0].append(k32.reshape(bp, tp, DA_HEADS, DA_VDIM))
        outs[1].append(v32.reshape(bp, tp, DA_HEADS, DA_VDIM))
        outs[2].append(s_p)
        x1, h2 = _outproj(o_da, o_gla, lw["wo_a"], lw["wo_b"], xp, g1, norm2_g[i], sc2, sh2)
        xp = _mlp(h2, lw["w_up"], lw["w_down"], x1, g2)

        sh1, sc1, g1, sh2, sc2, g2 = mods_s
        q, k32, k16, v32, v16, gall, glr = _mixer_inputs(xs, sc1, sh1, norm1_g[i], lw, q_norm_g[i], k_norm_g[i])
        o_da = _attn_sample(q.astype(F32).reshape(db, DA_HEADS, DA_VDIM),k32.reshape(db, DA_HEADS, DA_VDIM),
                            v32.reshape(db, DA_HEADS, DA_VDIM), cache_k[i], cache_v[i], page_table,
                            lams, diff_subln_g[i], lam_init)
        o_gla, s_s = _gla_sample(gall.reshape(db, 1, -1), glr.reshape(db, 1, LANES), lw["w2p"],
                                 gla_gate_b[i], gla_norm_g[i], state_gla[i])
        outs[3].append(k32.reshape(db, 1, DA_HEADS, DA_VDIM))
        outs[4].append(v32.reshape(db, 1, DA_HEADS, DA_VDIM))
        outs[5].append(s_s)
        x1, h2 = _outproj(o_da.reshape(1, db, da).astype(BF16), o_gla.reshape(1, db, -1).astype(BF16),
                          lw["wo_a"], lw["wo_b"], xs, g1, norm2_g[i], sc2, sh2)
        xs = _mlp(h2, lw["w_up"], lw["w_down"], x1, g2)

    return (xp, xs.reshape(db, 1, d), *[jnp.stack(o) for o in outs])
```

```python
import functools
import math

import jax
import jax.numpy as jnp
from jax import lax
from jax.experimental import pallas as pl
from jax.experimental.pallas import tpu as pltpu

F32 = jnp.float32
BF16 = jnp.bfloat16

EPS = 1e-6
DA_HEADS = 8
DA_VDIM = 128
DA_QKDIM = 64
GLA_HEADS = 4
GLA_KDIM = 128
GLA_VDIM = 256
GATE_RANK = 16
GATE_NORMALIZER = 16.0
GLA_CHUNK = 64
LANES = 128
Q_SCALE = DA_QKDIM ** -0.5 * math.log2(math.e)
DECODE_PAGES_PER_STEP = 16
ATTN_Q_BLOCK = 2048
ATTN_K_BLOCK = 512
ATTN_ROW_SUB = 256

VMEM_LIMIT = 56 * 1024 * 1024


def _params(*sem):
    return pltpu.CompilerParams(dimension_semantics=sem, vmem_limit_bytes=VMEM_LIMIT)


def _tile(n, t):
    t = min(n, t)
    assert n % t == 0, (n, t)
    return t


def _ada_kernel(c_ref, w_ref, b_ref, o_ref):
    c = c_ref[...]
    s = (c * jax.nn.sigmoid(c)).astype(BF16)
    o_ref[...] = jnp.dot(s, w_ref[...].astype(BF16), preferred_element_type=F32) + b_ref[...]


def _adaln(c, ada_w, ada_b):
    r, d = c.shape
    n = ada_w.shape[1]
    tn = _tile(n, 1024)
    return pl.pallas_call(
        _ada_kernel,
        out_shape=jax.ShapeDtypeStruct((r, n), F32),
        grid=(n // tn,),
        in_specs=[
            pl.BlockSpec((r, d), lambda j: (0, 0)),
            pl.BlockSpec((d, tn), lambda j: (0, j)),
            pl.BlockSpec((1, tn), lambda j: (0, j)),
        ],
        out_specs=pl.BlockSpec((r, tn), lambda j: (0, j)),
        compiler_params=_params("parallel"),
        name="adaln",
    )(c, ada_w, ada_b.reshape(1, n))


def _prenorm_kernel(x_ref, g_ref, sc_ref, sh_ref, o_ref):
    x = x_ref[0]
    ms = jnp.mean(x * x, axis=-1, keepdims=True)
    y = x * lax.rsqrt(ms + EPS) * g_ref[...]
    o_ref[0] = (y * (1.0 + sc_ref[0]) + sh_ref[0]).astype(o_ref.dtype)


def _mod_spec(mod, tt, d):
    if mod.shape[1] == 1:
        return pl.BlockSpec((1, 1, d), lambda b, t: (b, 0, 0))
    return pl.BlockSpec((1, tt, d), lambda b, t: (b, t, 0))


def _prenorm(x, g, scale, shift):
    b, t, d = x.shape
    tt = _tile(t, 512)
    return pl.pallas_call(
        _prenorm_kernel,
        out_shape=jax.ShapeDtypeStruct((b, t, d), BF16),
        grid=(b, t // tt),
        in_specs=[
            pl.BlockSpec((1, tt, d), lambda b, t: (b, t, 0)),
            pl.BlockSpec((1, d), lambda b, t: (0, 0)),
            _mod_spec(scale, tt, d),
            _mod_spec(shift, tt, d),
        ],
        out_specs=pl.BlockSpec((1, tt, d), lambda b, t: (b, t, 0)),
        compiler_params=_params("parallel", "parallel"),
        name="prenorm",
    )(x, g.reshape(1, d), scale, shift)


def _proj_kernel(*refs, norm, out_scale, n_out):
    if norm:
        h_ref, w_ref, g_ref, gm_ref = refs[:4]
        outs = refs[4:4 + n_out]
    else:
        h_ref, w_ref = refs[:2]
        outs = refs[2:2 + n_out]
    wb_ref = refs[-1]

    @pl.when(pl.program_id(1) == 0)
    def _():
        wb_ref[...] = w_ref[...].astype(BF16)

    acc = jnp.dot(h_ref[...], wb_ref[...], preferred_element_type=F32)
    if not norm:
        for o in outs:
            o[...] = acc.astype(o.dtype)
        return
    tn = acc.shape[1]
    for cb in range(tn // LANES):
        y = acc[:, cb * LANES:(cb + 1) * LANES]
        ss = jnp.dot((y * y).astype(BF16), gm_ref[...], preferred_element_type=F32)
        yn = y * lax.rsqrt(ss * (1.0 / DA_QKDIM) + EPS) * g_ref[...]
        for o, s in zip(outs, out_scale):
            o[:, cb * LANES:(cb + 1) * LANES] = (yn * s if s != 1.0 else yn).astype(o.dtype)


def _proj(h, w, out_dtypes, *, col0=0, n=None, norm_g=None, out_scale=None):
    m, k = h.shape
    n = w.shape[1] if n is None else n
    tm = _tile(m, 1024)
    tn = _tile(n, 1024)
    assert col0 % tn == 0
    cb0 = col0 // tn
    norm = norm_g is not None
    n_out = len(out_dtypes)
    if out_scale is None:
        out_scale = (1.0,) * n_out
    in_specs = [
        pl.BlockSpec((tm, k), lambda j, i: (i, 0)),
        pl.BlockSpec((k, tn), lambda j, i: (0, cb0 + j)),
    ]
    args = [h, w]
    if norm:
        g2 = jnp.concatenate([norm_g, norm_g]).reshape(1, LANES).astype(F32)
        lane = jnp.arange(LANES) // DA_QKDIM
        gm = (lane[:, None] == lane[None, :]).astype(BF16)
        in_specs += [
            pl.BlockSpec((1, LANES), lambda j, i: (0, 0)),
            pl.BlockSpec((LANES, LANES), lambda j, i: (0, 0)),
        ]
        args += [g2, gm]
    outs = pl.pallas_call(
        functools.partial(_proj_kernel, norm=norm, out_scale=tuple(out_scale), n_out=n_out),
        out_shape=[jax.ShapeDtypeStruct((m, n), dt) for dt in out_dtypes],
        grid=(n // tn, m // tm),
        in_specs=in_specs,
        out_specs=[pl.BlockSpec((tm, tn), lambda j, i: (i, j)) for _ in out_dtypes],
        scratch_shapes=[pltpu.VMEM((k, tn), BF16)],
        compiler_params=_params("parallel", "arbitrary"),
        name="proj_norm" if norm else "proj",
    )(*args)
    return outs


def _lambda_value(lq1, lk1, lq2, lk2, lam_init):
    a = jnp.sum(lq1[...] * lk1[...], axis=-1, keepdims=True)
    b = jnp.sum(lq2[...] * lk2[...], axis=-1, keepdims=True)
    return jnp.exp(a) - jnp.exp(b) + lam_init


def _split_halves(q):
    lane = lax.broadcasted_iota(jnp.int32, q.shape, 1)
    zero = jnp.zeros_like(q)
    return jnp.concatenate(
        [jnp.where(lane < DA_QKDIM, q, zero), jnp.where(lane >= DA_QKDIM, q, zero)], axis=0)


def _attn_kernel(q_ref, k_ref, v_ref, lq1, lk1, lq2, lk2, g_ref, o_ref,
                 vx_ref, q2_ref, m_ref, acc_ref, sa_ref, sb_ref, *, bq, bk, sub, lam_init):
    qi = pl.program_id(2)
    rows = 2 * bq
    ratio = bq // bk

    @pl.when(qi == 0)
    def _():
        vx_ref[:, :DA_VDIM] = v_ref[0]
        vx_ref[:, DA_VDIM:] = jnp.ones((vx_ref.shape[0], DA_VDIM), vx_ref.dtype)

    q2_ref[...] = _split_halves(q_ref[0])
    m_ref[...] = jnp.full_like(m_ref, -jnp.inf)
    acc_ref[...] = jnp.zeros_like(acc_ref)

    def n_cols(q_lo, diag):
        return bk if diag is None else max(0, min(bk, q_lo + sub - diag * bk))

    def scores(ki, s_ref, diag=None):
        start = pl.multiple_of(ki * bk, bk)
        if diag is None:
            s_ref[...] = lax.dot_general(q2_ref[...], k_ref[0, pl.ds(start, bk), :],
                                         (((1,), (1,)), ((), ())), preferred_element_type=F32)
            return
        for r in range(rows // sub):
            rs = slice(r * sub, (r + 1) * sub)
            ncol = n_cols((r * sub) % bq, diag)
            if ncol:
                k = k_ref[0, pl.ds(start, ncol), :]
                s_ref[rs, :ncol] = lax.dot_general(q2_ref[rs], k, (((1,), (1,)), ((), ())),
                                                   preferred_element_type=F32)

    def update(ki, s_ref, diag=None):
        start = pl.multiple_of(ki * bk, bk)
        for r in range(rows // sub):
            rs = slice(r * sub, (r + 1) * sub)
            q_lo = (r * sub) % bq
            ncol = n_cols(q_lo, diag)
            if not ncol:
                continue
            vx = vx_ref[pl.ds(start, ncol), :]
            s = s_ref[rs, :ncol]
            if diag is not None and diag * bk + ncol - 1 > q_lo:
                row = q_lo + lax.broadcasted_iota(jnp.int32, (sub, ncol), 0)
                col = diag * bk + lax.broadcasted_iota(jnp.int32, (sub, ncol), 1)
                s = jnp.where(col <= row, s, -jnp.inf)
            m_prev = m_ref[rs]
            m_new = jnp.maximum(m_prev, jnp.max(s, axis=1, keepdims=True))
            alpha = jnp.exp2(m_prev - m_new)
            p = jnp.exp2(s - jnp.tile(m_new, (1, ncol // LANES))).astype(BF16)
            acc_ref[rs] = (acc_ref[rs] * jnp.tile(alpha, (1, 2))
                           + jnp.dot(p, vx, preferred_element_type=F32))
            m_ref[rs] = m_new

    scores(0, sa_ref)

    def pair(j, carry):
        scores(2 * j + 1, sb_ref)
        update(2 * j, sa_ref)
        scores(2 * j + 2, sa_ref)
        update(2 * j + 1, sb_ref)
        return carry

    lax.fori_loop(0, qi * (ratio // 2), pair, 0)

    kd = qi * ratio
    bufs = (sa_ref, sb_ref)
    for d in range(ratio):
        if d + 1 < ratio:
            scores(kd + d + 1, bufs[(d + 1) % 2], diag=d + 1)
        update(kd + d, bufs[d % 2], diag=d)

    acc = acc_ref[...]
    o = acc[:, :DA_VDIM] / acc[:, DA_VDIM:]
    lam = _lambda_value(lq1, lk1, lq2, lk2, lam_init)
    od = o[:bq] - lam * o[bq:]
    ms = jnp.mean(od * od, axis=-1, keepdims=True)
    o_ref[0] = (od * lax.rsqrt(ms + EPS) * g_ref[...] * (1.0 - lam_init)).astype(o_ref.dtype)


def _attn_prompt(q, k, v, lams, subln_g, lam_init):
    b, t, w = q.shape
    h = w // DA_VDIM
    bq = _tile(t, ATTN_Q_BLOCK)
    bk = _tile(bq, ATTN_K_BLOCK)
    assert (bq // bk) % 2 == 0, "the two-buffer score pipeline consumes key blocks in pairs"
    sub = _tile(bk, ATTN_ROW_SUB)
    vec = pl.BlockSpec((1, DA_QKDIM), lambda b, h, i: (0, 0))
    return pl.pallas_call(
        functools.partial(_attn_kernel, bq=bq, bk=bk, sub=sub, lam_init=lam_init),
        out_shape=jax.ShapeDtypeStruct((b, t, w), BF16),
        grid=(b, h, t // bq),
        in_specs=[
            pl.BlockSpec((1, bq, DA_VDIM), lambda b, h, i: (b, i, h)),
            pl.BlockSpec((1, t, DA_VDIM), lambda b, h, i: (b, 0, h)),
            pl.BlockSpec((1, t, DA_VDIM), lambda b, h, i: (b, 0, h)),
            vec, vec, vec, vec,
            pl.BlockSpec((1, DA_VDIM), lambda b, h, i: (0, 0)),
        ],
        out_specs=pl.BlockSpec((1, bq, DA_VDIM), lambda b, h, i: (b, i, h)),
        scratch_shapes=[
            pltpu.VMEM((t, 2 * DA_VDIM), BF16),
            pltpu.VMEM((2 * bq, DA_VDIM), BF16),
            pltpu.VMEM((2 * bq, LANES), F32),
            pltpu.VMEM((2 * bq, 2 * DA_VDIM), F32),
            pltpu.VMEM((2 * bq, bk), F32),
            pltpu.VMEM((2 * bq, bk), F32),
        ],
        compiler_params=_params("parallel", "parallel", "arbitrary"),
        name="diff_attn_prompt",
    )(q, k, v, *[x.reshape(1, DA_QKDIM) for x in lams], subln_g.reshape(1, DA_VDIM))


def _log_sigmoid(z):
    return jnp.minimum(z, 0.0) - jnp.log1p(jnp.exp(-jnp.abs(z)))


def _row_to_col(row):
    n = row.shape[1]
    r = lax.broadcasted_iota(jnp.int32, (n, n), 0)
    c = lax.broadcasted_iota(jnp.int32, (n, n), 1)
    return jnp.sum(jnp.where(r == c, jnp.broadcast_to(row, (n, n)), 0.0), axis=1, keepdims=True)


def _split2(x):
    hi = x.astype(BF16)
    return hi, (x - hi.astype(F32)).astype(BF16)


def _dot_split(a, b):
    ah, al = _split2(a)
    bh, bl = _split2(b)
    return (jnp.dot(ah, bh, preferred_element_type=F32) + jnp.dot(ah, bl, preferred_element_type=F32)
            + jnp.dot(al, bh, preferred_element_type=F32))


def _gate_log(glr, w2, b):
    return _log_sigmoid(_dot_split(glr, w2) + b) * (1.0 / GATE_NORMALIZER)


def _gla_kernel(gq_ref, gkt_ref, gv_ref, go_ref, glr_ref, glrt_ref, w2_ref, w2t_ref, b_ref, bt_ref,
                ng_ref, o_ref, sfin_ref, s_ref, *, tt):
    ti = pl.program_id(2)

    @pl.when(ti == 0)
    def _():
        s_ref[...] = jnp.zeros_like(s_ref)

    c = GLA_CHUNK
    p2 = 2 * c
    g_all = _gate_log(glr_ref[0], w2_ref[...], b_ref[...])
    gt_all = _gate_log(w2t_ref[...], glrt_ref[0], bt_ref[...])
    r = lax.broadcasted_iota(jnp.int32, (p2, p2), 0)
    cc = lax.broadcasted_iota(jnp.int32, (p2, p2), 1)
    same = (r >= c) == (cc >= c)
    tril2 = same & (r >= cc)
    lane_lo = cc < c
    tril_b = tril2.astype(BF16)
    rhs_t = jnp.concatenate([same & (r <= cc), r < c, r >= c], axis=1).astype(BF16)
    s_prev = s_ref[...]
    for j in range(tt // p2):
        sl = slice(j * p2, (j + 1) * p2)
        gh, gl = _split2(g_all[sl])
        bc = (jnp.dot(tril_b, gh, preferred_element_type=F32)
              + jnp.dot(tril_b, gl, preferred_element_type=F32))
        th, tl = _split2(gt_all[:, sl])
        pre = (jnp.dot(th, rhs_t, preferred_element_type=F32)
               + jnp.dot(tl, rhs_t, preferred_element_type=F32))
        bc_t = pre[:, :p2]
        tot0 = pre[:, p2:2 * p2]
        tot1 = pre[:, 2 * p2:]
        q_t = (gq_ref[0, sl, :].astype(F32) * (GLA_KDIM ** -0.5) * jnp.exp(bc)).astype(BF16)
        k_all = gkt_ref[0, :, sl].astype(F32)
        k_t = (k_all * jnp.exp(-bc_t)).astype(BF16)
        k_e = k_all * jnp.exp(jnp.where(lane_lo, tot0, tot1) - bc_t)
        k_e0 = jnp.where(lane_lo, k_e, 0.0).astype(BF16)
        k_e1 = jnp.where(lane_lo, 0.0, k_e).astype(BF16)
        v = gv_ref[0, sl, :]
        att = jnp.dot(q_t, k_t, preferred_element_type=F32)
        att = jnp.where(tril2, att, 0.0).astype(BF16)
        o_intra = jnp.dot(att, v, preferred_element_type=F32)
        s_mid = jnp.tile(jnp.exp(tot0), (1, 2)) * s_prev + jnp.dot(k_e0, v, preferred_element_type=F32)
        s_next = jnp.tile(jnp.exp(tot1), (1, 2)) * s_mid + jnp.dot(k_e1, v, preferred_element_type=F32)
        o = jnp.concatenate([
            o_intra[:c] + jnp.dot(q_t[:c], s_prev.astype(BF16), preferred_element_type=F32),
            o_intra[c:] + jnp.dot(q_t[c:], s_mid.astype(BF16), preferred_element_type=F32),
        ], axis=0)
        s_prev = s_next
        ms = jnp.mean(o * o, axis=-1, keepdims=True)
        gate = go_ref[0, sl, :].astype(F32)
        gate = gate * jax.nn.sigmoid(gate)
        o_ref[0, sl, :] = (o * lax.rsqrt(ms + EPS) * ng_ref[...] * gate).astype(o_ref.dtype)
    s_ref[...] = s_prev

    @pl.when(ti == pl.num_programs(2) - 1)
    def _():
        sfin_ref[0, 0] = s_prev


def _gla_prompt(gall, gout, glr, w2p, gate_w2, gate_b, norm_g):
    b, t, _ = gall.shape
    tt = _tile(t, 512)
    assert tt % (2 * GLA_CHUNK) == 0
    hk = GLA_HEADS
    kw = GLA_HEADS * GLA_KDIM
    gk_t = jnp.swapaxes(gall[:, :, kw:2 * kw], 1, 2)
    glr_t = jnp.swapaxes(glr[:, :, :GATE_RANK], 1, 2)
    return pl.pallas_call(
        functools.partial(_gla_kernel, tt=tt),
        out_shape=[
            jax.ShapeDtypeStruct((b, t, GLA_HEADS * GLA_VDIM), BF16),
            jax.ShapeDtypeStruct((b, GLA_HEADS, GLA_KDIM, GLA_VDIM), F32),
        ],
        grid=(b, GLA_HEADS, t // tt),
        in_specs=[
            pl.BlockSpec((1, tt, GLA_KDIM), lambda b, h, i: (b, i, h)),
            pl.BlockSpec((1, GLA_KDIM, tt), lambda b, h, i: (b, h, i)),
            pl.BlockSpec((1, tt, GLA_VDIM), lambda b, h, i: (b, i, hk + h)),
            pl.BlockSpec((1, tt, GLA_VDIM), lambda b, h, i: (b, i, h)),
            pl.BlockSpec((1, tt, LANES), lambda b, h, i: (b, i, 0)),
            pl.BlockSpec((1, GATE_RANK, tt), lambda b, h, i: (b, 0, i)),
            pl.BlockSpec((LANES, GLA_KDIM), lambda b, h, i: (0, h)),
            pl.BlockSpec((GLA_KDIM, GATE_RANK), lambda b, h, i: (h, 0)),
            pl.BlockSpec((1, GLA_KDIM), lambda b, h, i: (0, h)),
            pl.BlockSpec((GLA_KDIM, 1), lambda b, h, i: (h, 0)),
            pl.BlockSpec((1, GLA_VDIM), lambda b, h, i: (0, 0)),
        ],
        out_specs=[
            pl.BlockSpec((1, tt, GLA_VDIM), lambda b, h, i: (b, i, h)),
            pl.BlockSpec((1, 1, GLA_KDIM, GLA_VDIM), lambda b, h, i: (b, h, 0, 0)),
        ],
        scratch_shapes=[pltpu.VMEM((GLA_KDIM, GLA_VDIM), F32)],
        compiler_params=_params("parallel", "parallel", "arbitrary"),
        name="gla_prompt",
    )(gall, gk_t, gall, gout, glr, glr_t, w2p, gate_w2.astype(F32).T, gate_b.reshape(1, -1),
      gate_b.reshape(-1, 1), norm_g.reshape(1, GLA_VDIM))


def _outproj_kernel(ma_ref, mb_ref, wa_ref, wb_ref, x_ref, g1_ref, ng_ref, sc_ref, sh_ref,
                    x1_ref, h2_ref):
    y = (jnp.dot(ma_ref[0], wa_ref[...], preferred_element_type=F32)
         + jnp.dot(mb_ref[0], wb_ref[...], preferred_element_type=F32))
    x1 = x_ref[0] + g1_ref[0] * y
    x1_ref[0] = x1
    ms = jnp.mean(x1 * x1, axis=-1, keepdims=True)
    h = x1 * lax.rsqrt(ms + EPS) * ng_ref[...]
    h2_ref[0] = (h * (1.0 + sc_ref[0]) + sh_ref[0]).astype(h2_ref.dtype)


def _outproj(mix_a, mix_b, w_a, w_b, x, gate1, norm2_g, scale2, shift2):
    b, t, d = x.shape
    ka = mix_a.shape[2]
    kb = mix_b.shape[2]
    tt = _tile(t, 512)
    row = lambda b, t: (b, t, 0)
    const = lambda b, t: (0, 0)
    return pl.pallas_call(
        _outproj_kernel,
        out_shape=[jax.ShapeDtypeStruct((b, t, d), F32), jax.ShapeDtypeStruct((b, t, d), BF16)],
        grid=(b, t // tt),
        in_specs=[
            pl.BlockSpec((1, tt, ka), row),
            pl.BlockSpec((1, tt, kb), row),
            pl.BlockSpec((ka, d), const),
            pl.BlockSpec((kb, d), const),
            pl.BlockSpec((1, tt, d), row),
            _mod_spec(gate1, tt, d),
            pl.BlockSpec((1, d), const),
            _mod_spec(scale2, tt, d),
            _mod_spec(shift2, tt, d),
        ],
        out_specs=[pl.BlockSpec((1, tt, d), row), pl.BlockSpec((1, tt, d), row)],
        compiler_params=_params("parallel", "parallel"),
        name="outproj",
    )(mix_a, mix_b, w_a, w_b, x, gate1, norm2_g.reshape(1, d), scale2, shift2)


def _mlp_kernel(h_ref, wu_ref, wd_ref, x_ref, g2_ref, o_ref, acc_ref):
    f = pl.program_id(2)

    @pl.when(f == 0)
    def _():
        acc_ref[...] = jnp.zeros_like(acc_ref)

    u = jnp.dot(h_ref[0], wu_ref[...], preferred_element_type=F32)
    u = jnp.maximum(u, 0.0)
    u = (u * u).astype(BF16)
    acc_ref[...] += jnp.dot(u, wd_ref[...], preferred_element_type=F32)

    @pl.when(f == pl.num_programs(2) - 1)
    def _():
        o_ref[0] = x_ref[0] + g2_ref[0] * acc_ref[...]


def _mlp(h2, w_up, w_down, x1, gate2):
    b, t, d = x1.shape
    ff = w_up.shape[1]
    tt = _tile(t, 512)
    tf = _tile(ff, 1024)
    if gate2.shape[1] == 1:
        g_spec = pl.BlockSpec((1, 1, d), lambda b, t, f: (b, 0, 0))
    else:
        g_spec = pl.BlockSpec((1, tt, d), lambda b, t, f: (b, t, 0))
    row = lambda b, t, f: (b, t, 0)
    return pl.pallas_call(
        _mlp_kernel,
        out_shape=jax.ShapeDtypeStruct((b, t, d), F32),
        grid=(b, t // tt, ff // tf),
        in_specs=[
            pl.BlockSpec((1, tt, d), row),
            pl.BlockSpec((d, tf), lambda b, t, f: (0, f)),
            pl.BlockSpec((tf, d), lambda b, t, f: (f, 0)),
            pl.BlockSpec((1, tt, d), row),
            g_spec,
        ],
        out_specs=pl.BlockSpec((1, tt, d), row),
        scratch_shapes=[pltpu.VMEM((tt, d), F32)],
        compiler_params=_params("parallel", "parallel", "arbitrary"),
        name="mlp",
    )(h2, w_up, w_down, x1, gate2)


def _decode_kernel(*refs, lam_init, n_grp):
    pt_ref, q_ref, kn_ref, vn_ref = refs[:4]
    ck_refs = refs[4:4 + n_grp]
    cv_refs = refs[4 + n_grp:4 + 2 * n_grp]
    lq1, lk1, lq2, lk2, g_ref, o_ref, m_ref, l_ref, acc_ref = refs[4 + 2 * n_grp:]
    del pt_ref
    p = pl.program_id(1)
    nh = DA_HEADS
    page = ck_refs[0].shape[0]
    cols = page * nh

    @pl.when(p == 0)
    def _():
        m_ref[...] = jnp.full_like(m_ref, -jnp.inf)
        l_ref[...] = jnp.zeros_like(l_ref)
        acc_ref[...] = jnp.zeros_like(acc_ref)

    q2 = _split_halves(q_ref[0]).astype(BF16)
    row_h = lax.broadcasted_iota(jnp.int32, (2 * nh, cols), 0) % nh
    col_h = lax.broadcasted_iota(jnp.int32, (2 * nh, cols), 1) % nh
    valid = row_h == col_h
    scores = []
    for ck_ref in ck_refs:
        kp = ck_ref[...].reshape(cols, DA_VDIM).astype(BF16)
        s = lax.dot_general(q2, kp, (((1,), (1,)), ((), ())), preferred_element_type=F32)
        scores.append(jnp.where(valid, s, -jnp.inf))
    m_prev = m_ref[...]
    m_new = m_prev
    for s in scores:
        m_new = jnp.maximum(m_new, jnp.max(s, axis=1, keepdims=True))
    alpha = jnp.exp2(m_prev - m_new)
    l = alpha * l_ref[...]
    acc = acc_ref[...] * alpha
    for s, cv_ref in zip(scores, cv_refs):
        pr = jnp.exp2(s - m_new)
        l = l + jnp.sum(pr, axis=1, keepdims=True)
        vp = cv_ref[...].reshape(cols, DA_VDIM).astype(BF16)
        acc = acc + jnp.dot(pr.astype(BF16), vp, preferred_element_type=F32)
    l_ref[...] = l
    acc_ref[...] = acc
    m_ref[...] = m_new

    @pl.when(p == pl.num_programs(1) - 1)
    def _():
        kn = kn_ref[0]
        vn = vn_ref[0]
        k2 = jnp.concatenate([kn, kn], axis=0)
        v2 = jnp.concatenate([vn, vn], axis=0)
        s_new = jnp.sum(q2.astype(F32) * k2, axis=1, keepdims=True)
        m_fin = jnp.maximum(m_new, s_new)
        a = jnp.exp2(m_new - m_fin)
        pn = jnp.exp2(s_new - m_fin)
        o = (acc * a + pn * v2) / (a * l + pn)
        lam = _lambda_value(lq1, lk1, lq2, lk2, lam_init)
        od = o[:nh] - lam * o[nh:]
        ms = jnp.mean(od * od, axis=-1, keepdims=True)
        o_ref[0] = od * lax.rsqrt(ms + EPS) * g_ref[...] * (1.0 - lam_init)


def _attn_sample(q, k_new, v_new, cache_k, cache_v, page_table, lams, subln_g, lam_init):
    db, nh, dv = q.shape
    n_pages = page_table.shape[1]
    page = cache_k.shape[1]
    n_grp = max(g for g in range(1, DECODE_PAGES_PER_STEP + 1) if n_pages % g == 0)
    tok = pl.BlockSpec((1, nh, dv), lambda b, p, pt: (b, 0, 0))
    cache = [pl.BlockSpec((None, page, nh, dv), lambda b, p, pt, j=j: (pt[b, p * n_grp + j], 0, 0, 0))
             for j in range(n_grp)]
    vec = pl.BlockSpec((1, DA_QKDIM), lambda b, p, pt: (0, 0))
    grid_spec = pltpu.PrefetchScalarGridSpec(
        num_scalar_prefetch=1,
        grid=(db, n_pages // n_grp),
        in_specs=[tok, tok, tok, *cache, *cache, vec, vec, vec, vec,
                  pl.BlockSpec((1, dv), lambda b, p, pt: (0, 0))],
        out_specs=pl.BlockSpec((1, nh, dv), lambda b, p, pt: (b, 0, 0)),
        scratch_shapes=[
            pltpu.VMEM((2 * nh, 1), F32),
            pltpu.VMEM((2 * nh, 1), F32),
            pltpu.VMEM((2 * nh, dv), F32),
        ],
    )
    return pl.pallas_call(
        functools.partial(_decode_kernel, lam_init=lam_init, n_grp=n_grp),
        out_shape=jax.ShapeDtypeStruct((db, nh, dv), F32),
        grid_spec=grid_spec,
        compiler_params=_params("parallel", "arbitrary"),
        name="diff_attn_sample",
    )(page_table, q, k_new, v_new, *([cache_k] * n_grp), *([cache_v] * n_grp),
      *[x.reshape(1, DA_QKDIM) for x in lams], subln_g.reshape(1, dv))


def _gla_step_kernel(g_ref, go_ref, glr_ref, w2_ref, b_ref, ng_ref, s0_ref, o_ref, s_ref):
    hk = GLA_HEADS * GLA_KDIM
    glr = jnp.broadcast_to(glr_ref[0], (8, LANES))
    g_all = _gate_log(glr, w2_ref[...], b_ref[...])[0:1]
    for h in range(GLA_HEADS):
        ks = slice(h * GLA_KDIM, (h + 1) * GLA_KDIM)
        q = g_ref[0, :, ks].astype(F32) * (GLA_KDIM ** -0.5)
        k = g_ref[0, :, hk + h * GLA_KDIM:hk + (h + 1) * GLA_KDIM].astype(F32)
        v = g_ref[0, :, 2 * hk + h * GLA_VDIM:2 * hk + (h + 1) * GLA_VDIM].astype(F32)
        gate = go_ref[0, :, h * GLA_VDIM:(h + 1) * GLA_VDIM].astype(F32)
        s = _row_to_col(jnp.exp(g_all[:, ks])) * s0_ref[h] + _row_to_col(k) * v
        s_ref[h] = s
        o = jnp.sum(_row_to_col(q) * s, axis=0, keepdims=True)
        ms = jnp.mean(o * o, axis=-1, keepdims=True)
        o_ref[0, :, h * GLA_VDIM:(h + 1) * GLA_VDIM] = (
            o * lax.rsqrt(ms + EPS) * ng_ref[...] * (gate * jax.nn.sigmoid(gate)))


def _gla_sample(gall, gout, glr, w2p, gate_b, norm_g, state):
    db = gall.shape[0]
    wg = gall.shape[2]
    const = lambda b: (0, 0)
    return pl.pallas_call(
        _gla_step_kernel,
        out_shape=[
            jax.ShapeDtypeStruct((db, 1, GLA_HEADS * GLA_VDIM), F32),
            jax.ShapeDtypeStruct(state.shape, F32),
        ],
        grid=(db,),
        in_specs=[
            pl.BlockSpec((1, 1, wg), lambda b: (b, 0, 0)),
            pl.BlockSpec((1, 1, gout.shape[2]), lambda b: (b, 0, 0)),
            pl.BlockSpec((1, 1, LANES), lambda b: (b, 0, 0)),
            pl.BlockSpec((LANES, GLA_HEADS * GLA_KDIM), const),
            pl.BlockSpec((1, GLA_HEADS * GLA_KDIM), const),
            pl.BlockSpec((1, GLA_VDIM), const),
            pl.BlockSpec((None, GLA_HEADS, GLA_KDIM, GLA_VDIM), lambda b: (b, 0, 0, 0)),
        ],
        out_specs=[
            pl.BlockSpec((1, 1, GLA_HEADS * GLA_VDIM), lambda b: (b, 0, 0)),
            pl.BlockSpec((None, GLA_HEADS, GLA_KDIM, GLA_VDIM), lambda b: (b, 0, 0, 0)),
        ],
        compiler_params=_params("parallel"),
        name="gla_sample",
    )(gall, gout, glr, w2p, gate_b.reshape(1, -1), norm_g.reshape(1, GLA_VDIM), state)


def _layer_weights(w_in, gate_w2, w_out, w_up, w_down):
    da = DA_HEADS * DA_VDIM
    gv = GLA_HEADS * GLA_VDIM
    c_glr = 3 * da + 2 * GLA_HEADS * GLA_KDIM + gv
    c_gout = c_glr + GATE_RANK
    w_glr = jnp.pad(w_in[:, c_glr:c_gout], ((0, 0), (0, LANES - GATE_RANK)))
    w2p = jnp.pad(gate_w2.astype(F32), ((0, LANES - GATE_RANK), (0, 0)))
    wo = w_out.astype(BF16)
    return dict(
        w_in=w_in, w_gout=w_in[:, c_gout:c_gout + gv], w_glr=w_glr, w2p=w2p,
        wo_a=wo[:da], wo_b=wo[da:], w_up=w_up.astype(BF16), w_down=w_down.astype(BF16))


def _mixer_inputs(x, scale1, shift1, norm1_g, lw, q_norm_g, k_norm_g):
    b, t, d = x.shape
    da = DA_HEADS * DA_VDIM
    n_gqkv = 2 * GLA_HEADS * GLA_KDIM + GLA_HEADS * GLA_VDIM
    h = _prenorm(x, norm1_g, scale1, shift1).reshape(b * t, d)
    w_in = lw["w_in"]
    (q,) = _proj(h, w_in, [BF16], col0=0, n=da, norm_g=q_norm_g, out_scale=(Q_SCALE,))
    k32, k16 = _proj(h, w_in, [F32, BF16], col0=da, n=da, norm_g=k_norm_g)
    v32, v16 = _proj(h, w_in, [F32, BF16], col0=2 * da, n=da)
    (gqkv,) = _proj(h, w_in, [BF16], col0=3 * da, n=n_gqkv)
    (gout,) = _proj(h, lw["w_gout"], [BF16])
    (glr,) = _proj(h, lw["w_glr"], [F32])
    return q, k32, k16, v32, v16, gqkv, gout, glr


def kernel(x_prompt, x_sample, cache_k, cache_v, state_gla, page_table, c_prompt, c_sample, norm1_g, ada_w, ada_b, w_in, q_norm_g, k_norm_g, lambda_q1, lambda_k1, lambda_q2, lambda_k2, diff_subln_g, gla_gate_w2, gla_gate_b, gla_norm_g, w_out, norm2_g, w_up, w_down):
    depth = w_in.shape[0]
    bp, tp, d = x_prompt.shape
    db, ts, _ = x_sample.shape
    assert ts == 1, "sample group decodes one token per sequence"
    da = DA_HEADS * DA_VDIM
    xp = x_prompt
    xs = x_sample.reshape(1, db, d)
    outs = [[] for _ in range(6)]
    n_ctl = bp + db
    n_pad = -n_ctl % 16
    c_all = jnp.pad(jnp.concatenate([c_prompt, c_sample], axis=0), ((0, n_pad), (0, 0)))
    for i in range(depth):
        lam_init = 0.8 - 0.6 * math.exp(-0.3 * i)
        lams = (lambda_q1[i], lambda_k1[i], lambda_q2[i], lambda_k2[i])
        lw = _layer_weights(w_in[i], gla_gate_w2[i], w_out[i], w_up[i], w_down[i])
        mod = _adaln(c_all, ada_w[i], ada_b[i])
        mods_p = [m[:bp, None, :] for m in jnp.split(mod, 6, axis=-1)]
        mods_s = [m[None, bp:bp + db, :] for m in jnp.split(mod, 6, axis=-1)]

        sh1, sc1, g1, sh2, sc2, g2 = mods_p
        q, k32, k16, v32, v16, gall, gout, glr = _mixer_inputs(
            xp, sc1, sh1, norm1_g[i], lw, q_norm_g[i], k_norm_g[i])
        o_da = _attn_prompt(q.reshape(bp, tp, da), k16.reshape(bp, tp, da), v16.reshape(bp, tp, da),
                            lams, diff_subln_g[i], lam_init)
        o_gla, s_p = _gla_prompt(gall.reshape(bp, tp, -1), gout.reshape(bp, tp, -1),
                                 glr.reshape(bp, tp, LANES), lw["w2p"],
                                 gla_gate_w2[i], gla_gate_b[i], gla_norm_g[i])
        outs[0].append(k32.reshape(bp, tp, DA_HEADS, DA_VDIM))
        outs[1].append(v32.reshape(bp, tp, DA_HEADS, DA_VDIM))
        outs[2].append(s_p)
        x1, h2 = _outproj(o_da, o_gla, lw["wo_a"], lw["wo_b"], xp, g1, norm2_g[i], sc2, sh2)
        xp = _mlp(h2, lw["w_up"], lw["w_down"], x1, g2)

        sh1, sc1, g1, sh2, sc2, g2 = mods_s
        q, k32, k16, v32, v16, gall, gout, glr = _mixer_inputs(
            xs, sc1, sh1, norm1_g[i], lw, q_norm_g[i], k_norm_g[i])
        o_da = _attn_sample(q.astype(F32).reshape(db, DA_HEADS, DA_VDIM), k32.reshape(db, DA_HEADS, DA_VDIM),
                            v32.reshape(db, DA_HEADS, DA_VDIM), cache_k[i], cache_v[i], page_table,
                            lams, diff_subln_g[i], lam_init)
        o_gla, s_s = _gla_sample(gall.reshape(db, 1, -1), gout.reshape(db, 1, -1),
                                 glr.reshape(db, 1, LANES), lw["w2p"],
                                 gla_gate_b[i], gla_norm_g[i], state_gla[i])
        outs[3].append(k32.reshape(db, 1, DA_HEADS, DA_VDIM))
        outs[4].append(v32.reshape(db, 1, DA_HEADS, DA_VDIM))
        outs[5].append(s_s)
        x1, h2 = _outproj(o_da.reshape(1, db, da).astype(BF16), o_gla.reshape(1, db, -1).astype(BF16),
                          lw["wo_a"], lw["wo_b"], xs, g1, norm2_g[i], sc2, sh2)
        xs = _mlp(h2, lw["w_up"], lw["w_down"], x1, g2)

    return (xp, xs.reshape(db, 1, d), *[jnp.stack(o) for o in outs])
```

```python
import functools
import math

import jax
import jax.numpy as jnp
from jax import lax
from jax.experimental import pallas as pl
from jax.experimental.pallas import tpu as pltpu

F32 = jnp.float32
BF16 = jnp.bfloat16

EPS = 1e-6
DA_HEADS = 8
DA_VDIM = 128
DA_QKDIM = 64
GLA_HEADS = 4
GLA_KDIM = 128
GLA_VDIM = 256
GATE_RANK = 16
GATE_NORMALIZER = 16.0
GLA_CHUNK = 64
LANES = 128
Q_SCALE = DA_QKDIM ** -0.5 * math.log2(math.e)
DECODE_PAGES_PER_STEP = 16
GLA_HEADS_PER_STEP = 4
ATTN_Q_BLOCK = 2048
ATTN_K_BLOCK = 512
ATTN_ROW_SUB = 256

VMEM_LIMIT = 56 * 1024 * 1024


def _params(*sem):
    return pltpu.CompilerParams(dimension_semantics=sem, vmem_limit_bytes=VMEM_LIMIT)


def _tile(n, t):
    t = min(n, t)
    assert n % t == 0, (n, t)
    return t


def _ada_kernel(c_ref, w_ref, b_ref, o_ref):
    c = c_ref[...]
    s = (c * jax.nn.sigmoid(c)).astype(BF16)
    o_ref[...] = jnp.dot(s, w_ref[...].astype(BF16), preferred_element_type=F32) + b_ref[...]


def _adaln(c, ada_w, ada_b):
    r, d = c.shape
    n = ada_w.shape[1]
    tn = _tile(n, 1024)
    return pl.pallas_call(
        _ada_kernel,
        out_shape=jax.ShapeDtypeStruct((r, n), F32),
        grid=(n // tn,),
        in_specs=[
            pl.BlockSpec((r, d), lambda j: (0, 0)),
            pl.BlockSpec((d, tn), lambda j: (0, j)),
            pl.BlockSpec((1, tn), lambda j: (0, j)),
        ],
        out_specs=pl.BlockSpec((r, tn), lambda j: (0, j)),
        compiler_params=_params("parallel"),
        name="adaln",
    )(c, ada_w, ada_b.reshape(1, n))


def _prenorm_kernel(x_ref, g_ref, sc_ref, sh_ref, o_ref):
    x = x_ref[0]
    ms = jnp.mean(x * x, axis=-1, keepdims=True)
    y = x * lax.rsqrt(ms + EPS) * g_ref[...]
    o_ref[0] = (y * (1.0 + sc_ref[0]) + sh_ref[0]).astype(o_ref.dtype)


def _mod_spec(mod, tt, d):
    if mod.shape[1] == 1:
        return pl.BlockSpec((1, 1, d), lambda b, t: (b, 0, 0))
    return pl.BlockSpec((1, tt, d), lambda b, t: (b, t, 0))


def _prenorm(x, g, scale, shift):
    b, t, d = x.shape
    tt = _tile(t, 512)
    return pl.pallas_call(
        _prenorm_kernel,
        out_shape=jax.ShapeDtypeStruct((b, t, d), BF16),
        grid=(b, t // tt),
        in_specs=[
            pl.BlockSpec((1, tt, d), lambda b, t: (b, t, 0)),
            pl.BlockSpec((1, d), lambda b, t: (0, 0)),
            _mod_spec(scale, tt, d),
            _mod_spec(shift, tt, d),
        ],
        out_specs=pl.BlockSpec((1, tt, d), lambda b, t: (b, t, 0)),
        compiler_params=_params("parallel", "parallel"),
        name="prenorm",
    )(x, g.reshape(1, d), scale, shift)


def _proj_kernel(*refs, norm, out_scale, n_out):
    if norm:
        h_ref, w_ref, g_ref, gm_ref = refs[:4]
        outs = refs[4:4 + n_out]
    else:
        h_ref, w_ref = refs[:2]
        outs = refs[2:2 + n_out]
    wb_ref = refs[-1]

    @pl.when(pl.program_id(1) == 0)
    def _():
        wb_ref[...] = w_ref[...].astype(BF16)

    acc = jnp.dot(h_ref[...], wb_ref[...], preferred_element_type=F32)
    if not norm:
        for o in outs:
            o[...] = acc.astype(o.dtype)
        return
    tn = acc.shape[1]
    for cb in range(tn // LANES):
        y = acc[:, cb * LANES:(cb + 1) * LANES]
        ss = jnp.dot((y * y).astype(BF16), gm_ref[...], preferred_element_type=F32)
        yn = y * lax.rsqrt(ss * (1.0 / DA_QKDIM) + EPS) * g_ref[...]
        for o, s in zip(outs, out_scale):
            o[:, cb * LANES:(cb + 1) * LANES] = (yn * s if s != 1.0 else yn).astype(o.dtype)


def _proj(h, w, out_dtypes, *, col0=0, n=None, norm_g=None, out_scale=None):
    m, k = h.shape
    n = w.shape[1] if n is None else n
    tm = _tile(m, 1024)
    tn = _tile(n, 1024)
    assert col0 % tn == 0
    cb0 = col0 // tn
    norm = norm_g is not None
    n_out = len(out_dtypes)
    if out_scale is None:
        out_scale = (1.0,) * n_out
    in_specs = [
        pl.BlockSpec((tm, k), lambda j, i: (i, 0)),
        pl.BlockSpec((k, tn), lambda j, i: (0, cb0 + j)),
    ]
    args = [h, w]
    if norm:
        g2 = jnp.concatenate([norm_g, norm_g]).reshape(1, LANES).astype(F32)
        lane = jnp.arange(LANES) // DA_QKDIM
        gm = (lane[:, None] == lane[None, :]).astype(BF16)
        in_specs += [
            pl.BlockSpec((1, LANES), lambda j, i: (0, 0)),
            pl.BlockSpec((LANES, LANES), lambda j, i: (0, 0)),
        ]
        args += [g2, gm]
    outs = pl.pallas_call(
        functools.partial(_proj_kernel, norm=norm, out_scale=tuple(out_scale), n_out=n_out),
        out_shape=[jax.ShapeDtypeStruct((m, n), dt) for dt in out_dtypes],
        grid=(n // tn, m // tm),
        in_specs=in_specs,
        out_specs=[pl.BlockSpec((tm, tn), lambda j, i: (i, j)) for _ in out_dtypes],
        scratch_shapes=[pltpu.VMEM((k, tn), BF16)],
        compiler_params=_params("parallel", "arbitrary"),
        name="proj_norm" if norm else "proj",
    )(*args)
    return outs


def _lambda_value(lq1, lk1, lq2, lk2, lam_init):
    a = jnp.sum(lq1[...] * lk1[...], axis=-1, keepdims=True)
    b = jnp.sum(lq2[...] * lk2[...], axis=-1, keepdims=True)
    return jnp.exp(a) - jnp.exp(b) + lam_init


def _split_halves(q):
    lane = lax.broadcasted_iota(jnp.int32, q.shape, 1)
    zero = jnp.zeros_like(q)
    return jnp.concatenate(
        [jnp.where(lane < DA_QKDIM, q, zero), jnp.where(lane >= DA_QKDIM, q, zero)], axis=0)


def _attn_kernel(q_ref, k_ref, v_ref, lq1, lk1, lq2, lk2, g_ref, o_ref,
                 vx_ref, q2_ref, m_ref, acc_ref, sa_ref, sb_ref, *, bq, bk, sub, lam_init):
    qi = pl.program_id(2)
    rows = 2 * bq
    ratio = bq // bk

    @pl.when(qi == 0)
    def _():
        vx_ref[:, :DA_VDIM] = v_ref[0]
        vx_ref[:, DA_VDIM:] = jnp.ones((vx_ref.shape[0], DA_VDIM), vx_ref.dtype)

    q2_ref[...] = _split_halves(q_ref[0])
    m_ref[...] = jnp.full_like(m_ref, -jnp.inf)
    acc_ref[...] = jnp.zeros_like(acc_ref)

    def n_cols(q_lo, diag):
        return bk if diag is None else max(0, min(bk, q_lo + sub - diag * bk))

    def scores(ki, s_ref, diag=None):
        start = pl.multiple_of(ki * bk, bk)
        if diag is None:
            s_ref[...] = lax.dot_general(q2_ref[...], k_ref[0, pl.ds(start, bk), :],
                                         (((1,), (1,)), ((), ())), preferred_element_type=F32)
            return
        for r in range(rows // sub):
            rs = slice(r * sub, (r + 1) * sub)
            ncol = n_cols((r * sub) % bq, diag)
            if ncol:
                k = k_ref[0, pl.ds(start, ncol), :]
                s_ref[rs, :ncol] = lax.dot_general(q2_ref[rs], k, (((1,), (1,)), ((), ())),
                                                   preferred_element_type=F32)

    def update(ki, s_ref, diag=None):
        start = pl.multiple_of(ki * bk, bk)
        for r in range(rows // sub):
            rs = slice(r * sub, (r + 1) * sub)
            q_lo = (r * sub) % bq
            ncol = n_cols(q_lo, diag)
            if not ncol:
                continue
            vx = vx_ref[pl.ds(start, ncol), :]
            s = s_ref[rs, :ncol]
            if diag is not None and diag * bk + ncol - 1 > q_lo:
                row = q_lo + lax.broadcasted_iota(jnp.int32, (sub, ncol), 0)
                col = diag * bk + lax.broadcasted_iota(jnp.int32, (sub, ncol), 1)
                s = jnp.where(col <= row, s, -jnp.inf)
            m_prev = m_ref[rs]
            m_new = jnp.maximum(m_prev, jnp.max(s, axis=1, keepdims=True))
            alpha = jnp.exp2(m_prev - m_new)
            p = jnp.exp2(s - jnp.tile(m_new, (1, ncol // LANES))).astype(BF16)
            acc_ref[rs] = (acc_ref[rs] * jnp.tile(alpha, (1, 2))
                           + jnp.dot(p, vx, preferred_element_type=F32))
            m_ref[rs] = m_new

    scores(0, sa_ref)

    def pair(j, carry):
        scores(2 * j + 1, sb_ref)
        update(2 * j, sa_ref)
        scores(2 * j + 2, sa_ref)
        update(2 * j + 1, sb_ref)
        return carry

    lax.fori_loop(0, qi * (ratio // 2), pair, 0)

    kd = qi * ratio
    bufs = (sa_ref, sb_ref)
    for d in range(ratio):
        if d + 1 < ratio:
            scores(kd + d + 1, bufs[(d + 1) % 2], diag=d + 1)
        update(kd + d, bufs[d % 2], diag=d)

    acc = acc_ref[...]
    o = acc[:, :DA_VDIM] / acc[:, DA_VDIM:]
    lam = _lambda_value(lq1, lk1, lq2, lk2, lam_init)
    od = o[:bq] - lam * o[bq:]
    ms = jnp.mean(od * od, axis=-1, keepdims=True)
    o_ref[0] = (od * lax.rsqrt(ms + EPS) * g_ref[...] * (1.0 - lam_init)).astype(o_ref.dtype)


def _attn_prompt(q, k, v, lams, subln_g, lam_init):
    b, t, w = q.shape
    h = w // DA_VDIM
    bq = _tile(t, ATTN_Q_BLOCK)
    bk = _tile(bq, ATTN_K_BLOCK)
    assert (bq // bk) % 2 == 0, "the two-buffer score pipeline consumes key blocks in pairs"
    sub = _tile(bk, ATTN_ROW_SUB)
    vec = pl.BlockSpec((1, DA_QKDIM), lambda b, h, i: (0, 0))
    return pl.pallas_call(
        functools.partial(_attn_kernel, bq=bq, bk=bk, sub=sub, lam_init=lam_init),
        out_shape=jax.ShapeDtypeStruct((b, t, w), BF16),
        grid=(b, h, t // bq),
        in_specs=[
            pl.BlockSpec((1, bq, DA_VDIM), lambda b, h, i: (b, i, h)),
            pl.BlockSpec((1, t, DA_VDIM), lambda b, h, i: (b, 0, h)),
            pl.BlockSpec((1, t, DA_VDIM), lambda b, h, i: (b, 0, h)),
            vec, vec, vec, vec,
            pl.BlockSpec((1, DA_VDIM), lambda b, h, i: (0, 0)),
        ],
        out_specs=pl.BlockSpec((1, bq, DA_VDIM), lambda b, h, i: (b, i, h)),
        scratch_shapes=[
            pltpu.VMEM((t, 2 * DA_VDIM), BF16),
            pltpu.VMEM((2 * bq, DA_VDIM), BF16),
            pltpu.VMEM((2 * bq, LANES), F32),
            pltpu.VMEM((2 * bq, 2 * DA_VDIM), F32),
            pltpu.VMEM((2 * bq, bk), F32),
            pltpu.VMEM((2 * bq, bk), F32),
        ],
        compiler_params=_params("parallel", "parallel", "arbitrary"),
        name="diff_attn_prompt",
    )(q, k, v, *[x.reshape(1, DA_QKDIM) for x in lams], subln_g.reshape(1, DA_VDIM))


def _log_sigmoid(z):
    return jnp.minimum(z, 0.0) - jnp.log(1.0 + jnp.exp(-jnp.abs(z)))


def _row_to_col(row):
    n = row.shape[1]
    r = lax.broadcasted_iota(jnp.int32, (n, n), 0)
    c = lax.broadcasted_iota(jnp.int32, (n, n), 1)
    return jnp.sum(jnp.where(r == c, jnp.broadcast_to(row, (n, n)), 0.0), axis=1, keepdims=True)


def _split2(x):
    hi = x.astype(BF16)
    return hi, (x - hi.astype(F32)).astype(BF16)


def _dot_split(a, b):
    ah, al = _split2(a)
    bh, bl = _split2(b)
    return (jnp.dot(ah, bh, preferred_element_type=F32) + jnp.dot(ah, bl, preferred_element_type=F32)
            + jnp.dot(al, bh, preferred_element_type=F32))


def _gate_log(glr, w2, b):
    return _log_sigmoid(_dot_split(glr, w2) + b) * (1.0 / GATE_NORMALIZER)


def _gla_kernel(gq_ref, gk_ref, gv_ref, go_ref, glr_ref, w2_ref, b_ref,
                ng_ref, o_ref, sfin_ref, s_ref, *, tt, hg):
    ti = pl.program_id(2)

    @pl.when(ti == 0)
    def _():
        s_ref[...] = jnp.zeros_like(s_ref)

    c = GLA_CHUNK
    p2 = 2 * c
    kd, vd = GLA_KDIM, GLA_VDIM
    r = lax.broadcasted_iota(jnp.int32, (p2, p2), 0)
    cc = lax.broadcasted_iota(jnp.int32, (p2, p2), 1)
    same = (r >= c) == (cc >= c)
    tril2 = same & (r >= cc)
    lane_lo = cc < c
    tril_b = tril2.astype(BF16)
    rhs_t = jnp.concatenate([same & (r <= cc), r < c, r >= c], axis=1).astype(BF16)
    g_all, state = [], []
    for hh in range(hg):
        ks = slice(hh * kd, (hh + 1) * kd)
        g_all.append(_gate_log(glr_ref[0], w2_ref[:, ks], b_ref[:, ks]))
        state.append(s_ref[hh])
    for j in range(tt // p2):
        sl = slice(j * p2, (j + 1) * p2)
        for hh in range(hg):
            ks = slice(hh * kd, (hh + 1) * kd)
            vs = slice(hh * vd, (hh + 1) * vd)
            s_prev = state[hh]
            gh, gl = _split2(g_all[hh][sl])
            bc = (jnp.dot(tril_b, gh, preferred_element_type=F32)
                  + jnp.dot(tril_b, gl, preferred_element_type=F32))
            th, tl = _split2(g_all[hh][sl].T)
            pre = (jnp.dot(th, rhs_t, preferred_element_type=F32)
                   + jnp.dot(tl, rhs_t, preferred_element_type=F32))
            bc_t = pre[:, :p2]
            tot0 = pre[:, p2:2 * p2]
            tot1 = pre[:, 2 * p2:]
            q_t = (gq_ref[0, sl, ks].astype(F32) * (kd ** -0.5) * jnp.exp(bc)).astype(BF16)
            k_all = gk_ref[0, sl, ks].astype(F32).T
            k_t = (k_all * jnp.exp(-bc_t)).astype(BF16)
            k_e = k_all * jnp.exp(jnp.where(lane_lo, tot0, tot1) - bc_t)
            k_e0 = jnp.where(lane_lo, k_e, 0.0).astype(BF16)
            k_e1 = jnp.where(lane_lo, 0.0, k_e).astype(BF16)
            v = gv_ref[0, sl, vs]
            att = jnp.dot(q_t, k_t, preferred_element_type=F32)
            att = jnp.where(tril2, att, 0.0).astype(BF16)
            o_intra = jnp.dot(att, v, preferred_element_type=F32)
            s_mid = (jnp.tile(jnp.exp(tot0), (1, 2)) * s_prev
                     + jnp.dot(k_e0, v, preferred_element_type=F32))
            state[hh] = (jnp.tile(jnp.exp(tot1), (1, 2)) * s_mid
                         + jnp.dot(k_e1, v, preferred_element_type=F32))
            o = jnp.concatenate([
                o_intra[:c] + jnp.dot(q_t[:c], s_prev.astype(BF16), preferred_element_type=F32),
                o_intra[c:] + jnp.dot(q_t[c:], s_mid.astype(BF16), preferred_element_type=F32),
            ], axis=0)
            ms = jnp.mean(o * o, axis=-1, keepdims=True)
            gate = go_ref[0, sl, vs].astype(F32)
            gate = gate * jax.nn.sigmoid(gate)
            o_ref[0, sl, vs] = (o * lax.rsqrt(ms + EPS) * ng_ref[...] * gate).astype(o_ref.dtype)
    for hh in range(hg):
        s_ref[hh] = state[hh]

    @pl.when(ti == pl.num_programs(2) - 1)
    def _():
        for hh in range(hg):
            sfin_ref[0, hh] = state[hh]


def _gla_prompt(gall, gout, glr, w2p, gate_b, norm_g):
    b, t, _ = gall.shape
    tt = _tile(t, 512)
    assert tt % (2 * GLA_CHUNK) == 0
    hg = GLA_HEADS_PER_STEP
    kd, vd = hg * GLA_KDIM, hg * GLA_VDIM
    ng = GLA_HEADS // hg
    return pl.pallas_call(
        functools.partial(_gla_kernel, tt=tt, hg=hg),
        out_shape=[
            jax.ShapeDtypeStruct((b, t, GLA_HEADS * GLA_VDIM), BF16),
            jax.ShapeDtypeStruct((b, GLA_HEADS, GLA_KDIM, GLA_VDIM), F32),
        ],
        grid=(b, ng, t // tt),
        in_specs=[
            pl.BlockSpec((1, tt, kd), lambda b, h, i: (b, i, h)),
            pl.BlockSpec((1, tt, kd), lambda b, h, i: (b, i, ng + h)),
            pl.BlockSpec((1, tt, vd), lambda b, h, i: (b, i, ng + h)),
            pl.BlockSpec((1, tt, vd), lambda b, h, i: (b, i, h)),
            pl.BlockSpec((1, tt, LANES), lambda b, h, i: (b, i, 0)),
            pl.BlockSpec((LANES, kd), lambda b, h, i: (0, h)),
            pl.BlockSpec((1, kd), lambda b, h, i: (0, h)),
            pl.BlockSpec((1, GLA_VDIM), lambda b, h, i: (0, 0)),
        ],
        out_specs=[
            pl.BlockSpec((1, tt, vd), lambda b, h, i: (b, i, h)),
            pl.BlockSpec((1, hg, GLA_KDIM, GLA_VDIM), lambda b, h, i: (b, h, 0, 0)),
        ],
        scratch_shapes=[pltpu.VMEM((hg, GLA_KDIM, GLA_VDIM), F32)],
        compiler_params=_params("parallel", "parallel", "arbitrary"),
        name="gla_prompt",
    )(gall, gall, gall, gout, glr, w2p, gate_b.reshape(1, -1), norm_g.reshape(1, GLA_VDIM))


def _outproj_kernel(ma_ref, mb_ref, wa_ref, wb_ref, x_ref, g1_ref, ng_ref, sc_ref, sh_ref,
                    x1_ref, h2_ref):
    y = (jnp.dot(ma_ref[0], wa_ref[...], preferred_element_type=F32)
         + jnp.dot(mb_ref[0], wb_ref[...], preferred_element_type=F32))
    x1 = x_ref[0] + g1_ref[0] * y
    x1_ref[0] = x1
    ms = jnp.mean(x1 * x1, axis=-1, keepdims=True)
    h = x1 * lax.rsqrt(ms + EPS) * ng_ref[...]
    h2_ref[0] = (h * (1.0 + sc_ref[0]) + sh_ref[0]).astype(h2_ref.dtype)


def _outproj(mix_a, mix_b, w_a, w_b, x, gate1, norm2_g, scale2, shift2):
    b, t, d = x.shape
    ka = mix_a.shape[2]
    kb = mix_b.shape[2]
    tt = _tile(t, 512)
    row = lambda b, t: (b, t, 0)
    const = lambda b, t: (0, 0)
    return pl.pallas_call(
        _outproj_kernel,
        out_shape=[jax.ShapeDtypeStruct((b, t, d), F32), jax.ShapeDtypeStruct((b, t, d), BF16)],
        grid=(b, t // tt),
        in_specs=[
            pl.BlockSpec((1, tt, ka), row),
            pl.BlockSpec((1, tt, kb), row),
            pl.BlockSpec((ka, d), const),
            pl.BlockSpec((kb, d), const),
            pl.BlockSpec((1, tt, d), row),
            _mod_spec(gate1, tt, d),
            pl.BlockSpec((1, d), const),
            _mod_spec(scale2, tt, d),
            _mod_spec(shift2, tt, d),
        ],
        out_specs=[pl.BlockSpec((1, tt, d), row), pl.BlockSpec((1, tt, d), row)],
        compiler_params=_params("parallel", "parallel"),
        name="outproj",
    )(mix_a, mix_b, w_a, w_b, x, gate1, norm2_g.reshape(1, d), scale2, shift2)


def _mlp_kernel(h_ref, wu_ref, wd_ref, x_ref, g2_ref, o_ref, acc_ref):
    f = pl.program_id(2)

    @pl.when(f == 0)
    def _():
        acc_ref[...] = jnp.zeros_like(acc_ref)

    u = jnp.dot(h_ref[0], wu_ref[...], preferred_element_type=F32)
    u = jnp.maximum(u, 0.0)
    u = (u * u).astype(BF16)
    acc_ref[...] += jnp.dot(u, wd_ref[...], preferred_element_type=F32)

    @pl.when(f == pl.num_programs(2) - 1)
    def _():
        o_ref[0] = x_ref[0] + g2_ref[0] * acc_ref[...]


def _mlp(h2, w_up, w_down, x1, gate2):
    b, t, d = x1.shape
    ff = w_up.shape[1]
    tt = _tile(t, 512)
    tf = _tile(ff, 1024)
    if gate2.shape[1] == 1:
        g_spec = pl.BlockSpec((1, 1, d), lambda b, t, f: (b, 0, 0))
    else:
        g_spec = pl.BlockSpec((1, tt, d), lambda b, t, f: (b, t, 0))
    row = lambda b, t, f: (b, t, 0)
    return pl.pallas_call(
        _mlp_kernel,
        out_shape=jax.ShapeDtypeStruct((b, t, d), F32),
        grid=(b, t // tt, ff // tf),
        in_specs=[
            pl.BlockSpec((1, tt, d), row),
            pl.BlockSpec((d, tf), lambda b, t, f: (0, f)),
            pl.BlockSpec((tf, d), lambda b, t, f: (f, 0)),
            pl.BlockSpec((1, tt, d), row),
            g_spec,
        ],
        out_specs=pl.BlockSpec((1, tt, d), row),
        scratch_shapes=[pltpu.VMEM((tt, d), F32)],
        compiler_params=_params("parallel", "parallel", "arbitrary"),
        name="mlp",
    )(h2, w_up, w_down, x1, gate2)


def _decode_kernel(*refs, lam_init, n_grp):
    pt_ref, q_ref, kn_ref, vn_ref = refs[:4]
    ck_refs = refs[4:4 + n_grp]
    cv_refs = refs[4 + n_grp:4 + 2 * n_grp]
    lq1, lk1, lq2, lk2, g_ref, o_ref, m_ref, l_ref, acc_ref = refs[4 + 2 * n_grp:]
    del pt_ref
    p = pl.program_id(1)
    nh = DA_HEADS
    page = ck_refs[0].shape[0]
    cols = page * nh

    @pl.when(p == 0)
    def _():
        m_ref[...] = jnp.full_like(m_ref, -jnp.inf)
        l_ref[...] = jnp.zeros_like(l_ref)
        acc_ref[...] = jnp.zeros_like(acc_ref)

    q2 = _split_halves(q_ref[0]).astype(BF16)
    row_h = lax.broadcasted_iota(jnp.int32, (2 * nh, cols), 0) % nh
    col_h = lax.broadcasted_iota(jnp.int32, (2 * nh, cols), 1) % nh
    valid = row_h == col_h
    scores = []
    for ck_ref in ck_refs:
        kp = ck_ref[...].reshape(cols, DA_VDIM).astype(BF16)
        s = lax.dot_general(q2, kp, (((1,), (1,)), ((), ())), preferred_element_type=F32)
        scores.append(jnp.where(valid, s, -jnp.inf))
    m_prev = m_ref[...]
    m_new = m_prev
    for s in scores:
        m_new = jnp.maximum(m_new, jnp.max(s, axis=1, keepdims=True))
    alpha = jnp.exp2(m_prev - m_new)
    l = alpha * l_ref[...]
    acc = acc_ref[...] * alpha
    for s, cv_ref in zip(scores, cv_refs):
        pr = jnp.exp2(s - m_new)
        l = l + jnp.sum(pr, axis=1, keepdims=True)
        vp = cv_ref[...].reshape(cols, DA_VDIM).astype(BF16)
        acc = acc + jnp.dot(pr.astype(BF16), vp, preferred_element_type=F32)
    l_ref[...] = l
    acc_ref[...] = acc
    m_ref[...] = m_new

    @pl.when(p == pl.num_programs(1) - 1)
    def _():
        kn = kn_ref[0]
        vn = vn_ref[0]
        k2 = jnp.concatenate([kn, kn], axis=0)
        v2 = jnp.concatenate([vn, vn], axis=0)
        s_new = jnp.sum(q2.astype(F32) * k2, axis=1, keepdims=True)
        m_fin = jnp.maximum(m_new, s_new)
        a = jnp.exp2(m_new - m_fin)
        pn = jnp.exp2(s_new - m_fin)
        o = (acc * a + pn * v2) / (a * l + pn)
        lam = _lambda_value(lq1, lk1, lq2, lk2, lam_init)
        od = o[:nh] - lam * o[nh:]
        ms = jnp.mean(od * od, axis=-1, keepdims=True)
        o_ref[0] = od * lax.rsqrt(ms + EPS) * g_ref[...] * (1.0 - lam_init)


def _attn_sample(q, k_new, v_new, cache_k, cache_v, page_table, lams, subln_g, lam_init):
    db, nh, dv = q.shape
    n_pages = page_table.shape[1]
    page = cache_k.shape[1]
    n_grp = max(g for g in range(1, DECODE_PAGES_PER_STEP + 1) if n_pages % g == 0)
    tok = pl.BlockSpec((1, nh, dv), lambda b, p, pt: (b, 0, 0))
    cache = [pl.BlockSpec((None, page, nh, dv), lambda b, p, pt, j=j: (pt[b, p * n_grp + j], 0, 0, 0))
             for j in range(n_grp)]
    vec = pl.BlockSpec((1, DA_QKDIM), lambda b, p, pt: (0, 0))
    grid_spec = pltpu.PrefetchScalarGridSpec(
        num_scalar_prefetch=1,
        grid=(db, n_pages // n_grp),
        in_specs=[tok, tok, tok, *cache, *cache, vec, vec, vec, vec,
                  pl.BlockSpec((1, dv), lambda b, p, pt: (0, 0))],
        out_specs=pl.BlockSpec((1, nh, dv), lambda b, p, pt: (b, 0, 0)),
        scratch_shapes=[
            pltpu.VMEM((2 * nh, 1), F32),
            pltpu.VMEM((2 * nh, 1), F32),
            pltpu.VMEM((2 * nh, dv), F32),
        ],
    )
    return pl.pallas_call(
        functools.partial(_decode_kernel, lam_init=lam_init, n_grp=n_grp),
        out_shape=jax.ShapeDtypeStruct((db, nh, dv), F32),
        grid_spec=grid_spec,
        compiler_params=_params("parallel", "arbitrary"),
        name="diff_attn_sample",
    )(page_table, q, k_new, v_new, *([cache_k] * n_grp), *([cache_v] * n_grp),
      *[x.reshape(1, DA_QKDIM) for x in lams], subln_g.reshape(1, dv))


def _gla_step_kernel(g_ref, go_ref, glr_ref, w2_ref, b_ref, ng_ref, s0_ref, o_ref, s_ref):
    hk = GLA_HEADS * GLA_KDIM
    glr = jnp.broadcast_to(glr_ref[0], (8, LANES))
    g_all = _gate_log(glr, w2_ref[...], b_ref[...])[0:1]
    for h in range(GLA_HEADS):
        ks = slice(h * GLA_KDIM, (h + 1) * GLA_KDIM)
        q = g_ref[0, :, ks].astype(F32) * (GLA_KDIM ** -0.5)
        k = g_ref[0, :, hk + h * GLA_KDIM:hk + (h + 1) * GLA_KDIM].astype(F32)
        v = g_ref[0, :, 2 * hk + h * GLA_VDIM:2 * hk + (h + 1) * GLA_VDIM].astype(F32)
        gate = go_ref[0, :, h * GLA_VDIM:(h + 1) * GLA_VDIM].astype(F32)
        s = _row_to_col(jnp.exp(g_all[:, ks])) * s0_ref[h] + _row_to_col(k) * v
        s_ref[h] = s
        o = jnp.sum(_row_to_col(q) * s, axis=0, keepdims=True)
        ms = jnp.mean(o * o, axis=-1, keepdims=True)
        o_ref[0, :, h * GLA_VDIM:(h + 1) * GLA_VDIM] = (
            o * lax.rsqrt(ms + EPS) * ng_ref[...] * (gate * jax.nn.sigmoid(gate)))


def _gla_sample(gall, gout, glr, w2p, gate_b, norm_g, state):
    db = gall.shape[0]
    wg = gall.shape[2]
    const = lambda b: (0, 0)
    return pl.pallas_call(
        _gla_step_kernel,
        out_shape=[
            jax.ShapeDtypeStruct((db, 1, GLA_HEADS * GLA_VDIM), F32),
            jax.ShapeDtypeStruct(state.shape, F32),
        ],
        grid=(db,),
        in_specs=[
            pl.BlockSpec((1, 1, wg), lambda b: (b, 0, 0)),
            pl.BlockSpec((1, 1, gout.shape[2]), lambda b: (b, 0, 0)),
            pl.BlockSpec((1, 1, LANES), lambda b: (b, 0, 0)),
            pl.BlockSpec((LANES, GLA_HEADS * GLA_KDIM), const),
            pl.BlockSpec((1, GLA_HEADS * GLA_KDIM), const),
            pl.BlockSpec((1, GLA_VDIM), const),
            pl.BlockSpec((None, GLA_HEADS, GLA_KDIM, GLA_VDIM), lambda b: (b, 0, 0, 0)),
        ],
        out_specs=[
            pl.BlockSpec((1, 1, GLA_HEADS * GLA_VDIM), lambda b: (b, 0, 0)),
            pl.BlockSpec((None, GLA_HEADS, GLA_KDIM, GLA_VDIM), lambda b: (b, 0, 0, 0)),
        ],
        compiler_params=_params("parallel"),
        name="gla_sample",
    )(gall, gout, glr, w2p, gate_b.reshape(1, -1), norm_g.reshape(1, GLA_VDIM), state)


def _layer_weights(w_in, gate_w2, w_out, w_up, w_down):
    da = DA_HEADS * DA_VDIM
    gv = GLA_HEADS * GLA_VDIM
    c_glr = 3 * da + 2 * GLA_HEADS * GLA_KDIM + gv
    c_gout = c_glr + GATE_RANK
    w_glr = jnp.pad(w_in[:, c_glr:c_gout], ((0, 0), (0, LANES - GATE_RANK)))
    w2p = jnp.pad(gate_w2.astype(F32), ((0, LANES - GATE_RANK), (0, 0)))
    wo = w_out.astype(BF16)
    return dict(
        w_in=w_in, w_gout=w_in[:, c_gout:c_gout + gv], w_glr=w_glr, w2p=w2p,
        wo_a=wo[:da], wo_b=wo[da:], w_up=w_up.astype(BF16), w_down=w_down.astype(BF16))


def _mixer_inputs(x, scale1, shift1, norm1_g, lw, q_norm_g, k_norm_g):
    b, t, d = x.shape
    da = DA_HEADS * DA_VDIM
    n_gqkv = 2 * GLA_HEADS * GLA_KDIM + GLA_HEADS * GLA_VDIM
    h = _prenorm(x, norm1_g, scale1, shift1).reshape(b * t, d)
    w_in = lw["w_in"]
    (q,) = _proj(h, w_in, [BF16], col0=0, n=da, norm_g=q_norm_g, out_scale=(Q_SCALE,))
    k32, k16 = _proj(h, w_in, [F32, BF16], col0=da, n=da, norm_g=k_norm_g)
    v32, v16 = _proj(h, w_in, [F32, BF16], col0=2 * da, n=da)
    (gqkv,) = _proj(h, w_in, [BF16], col0=3 * da, n=n_gqkv)
    (gout,) = _proj(h, lw["w_gout"], [BF16])
    (glr,) = _proj(h, lw["w_glr"], [F32])
    return q, k32, k16, v32, v16, gqkv, gout, glr


def kernel(x_prompt, x_sample, cache_k, cache_v, state_gla, page_table, c_prompt, c_sample, norm1_g, ada_w, ada_b, w_in, q_norm_g, k_norm_g, lambda_q1, lambda_k1, lambda_q2, lambda_k2, diff_subln_g, gla_gate_w2, gla_gate_b, gla_norm_g, w_out, norm2_g, w_up, w_down):
    depth = w_in.shape[0]
    bp, tp, d = x_prompt.shape
    db, ts, _ = x_sample.shape
    assert ts == 1, "sample group decodes one token per sequence"
    da = DA_HEADS * DA_VDIM
    xp = x_prompt
    xs = x_sample.reshape(1, db, d)
    outs = [[] for _ in range(6)]
    n_ctl = bp + db
    n_pad = -n_ctl % 16
    c_all = jnp.pad(jnp.concatenate([c_prompt, c_sample], axis=0), ((0, n_pad), (0, 0)))
    for i in range(depth):
        lam_init = 0.8 - 0.6 * math.exp(-0.3 * i)
        lams = (lambda_q1[i], lambda_k1[i], lambda_q2[i], lambda_k2[i])
        lw = _layer_weights(w_in[i], gla_gate_w2[i], w_out[i], w_up[i], w_down[i])
        mod = _adaln(c_all, ada_w[i], ada_b[i])
        mods_p = [m[:bp, None, :] for m in jnp.split(mod, 6, axis=-1)]
        mods_s = [m[None, bp:bp + db, :] for m in jnp.split(mod, 6, axis=-1)]

        sh1, sc1, g1, sh2, sc2, g2 = mods_p
        q, k32, k16, v32, v16, gall, gout, glr = _mixer_inputs(
            xp, sc1, sh1, norm1_g[i], lw, q_norm_g[i], k_norm_g[i])
        o_da = _attn_prompt(q.reshape(bp, tp, da), k16.reshape(bp, tp, da), v16.reshape(bp, tp, da),
                            lams, diff_subln_g[i], lam_init)
        o_gla, s_p = _gla_prompt(gall.reshape(bp, tp, -1), gout.reshape(bp, tp, -1),
                                 glr.reshape(bp, tp, LANES), lw["w2p"], gla_gate_b[i], gla_norm_g[i])
        outs[0].append(k32.reshape(bp, tp, DA_HEADS, DA_VDIM))
        outs[1].append(v32.reshape(bp, tp, DA_HEADS, DA_VDIM))
        outs[2].append(s_p)
        x1, h2 = _outproj(o_da, o_gla, lw["wo_a"], lw["wo_b"], xp, g1, norm2_g[i], sc2, sh2)
        xp = _mlp(h2, lw["w_up"], lw["w_down"], x1, g2)

        sh1, sc1, g1, sh2, sc2, g2 = mods_s
        q, k32, k16, v32, v16, gall, gout, glr = _mixer_inputs(
            xs, sc1, sh1, norm1_g[i], lw, q_norm_g[i], k_norm_g[i])
        o_da = _attn_sample(q.astype(F32).reshape(db, DA_HEADS, DA_VDIM), k32.reshape(db, DA_HEADS, DA_VDIM),
                            v32.reshape(db, DA_HEADS, DA_VDIM), cache_k[i], cache_v[i], page_table,
                            lams, diff_subln_g[i], lam_init)
        o_gla, s_s = _gla_sample(gall.reshape(db, 1, -1), gout.reshape(db, 1, -1),
                                 glr.reshape(db, 1, LANES), lw["w2p"],
                                 gla_gate_b[i], gla_norm_g[i], state_gla[i])
        outs[3].append(k32.reshape(db, 1, DA_HEADS, DA_VDIM))
        outs[4].append(v32.reshape(db, 1, DA_HEADS, DA_VDIM))
        outs[5].append(s_s)
        x1, h2 = _outproj(o_da.reshape(1, db, da).astype(BF16), o_gla.reshape(1, db, -1).astype(BF16),
                          lw["wo_a"], lw["wo_b"], xs, g1, norm2_g[i], sc2, sh2)
        xs = _mlp(h2, lw["w_up"], lw["w_down"], x1, g2)

    return (xp, xs.reshape(db, 1, d), *[jnp.stack(o) for o in outs])
```

```python
import functools
import math

import jax
import jax.numpy as jnp
from jax import lax
from jax.experimental import pallas as pl
from jax.experimental.pallas import tpu as pltpu

F32 = jnp.float32
BF16 = jnp.bfloat16

EPS = 1e-6
DA_HEADS = 8
DA_VDIM = 128
DA_QKDIM = 64
GLA_HEADS = 4
GLA_KDIM = 128
GLA_VDIM = 256
GATE_RANK = 16
GATE_NORMALIZER = 16.0
GLA_CHUNK = 64
LANES = 128
Q_SCALE = DA_QKDIM ** -0.5 * math.log2(math.e)
DECODE_PAGES_PER_STEP = 16
GLA_HEADS_PER_STEP = 4
ATTN_Q_BLOCK = 2048
ATTN_K_BLOCK = 512
ATTN_ROW_SUB = 256

VMEM_LIMIT = 56 * 1024 * 1024


def _params(*sem):
    return pltpu.CompilerParams(dimension_semantics=sem, vmem_limit_bytes=VMEM_LIMIT)


def _tile(n, t):
    t = min(n, t)
    assert n % t == 0, (n, t)
    return t


def _ada_kernel(c_ref, w_ref, b_ref, o_ref):
    c = c_ref[...]
    s = (c * jax.nn.sigmoid(c)).astype(BF16)
    o_ref[...] = jnp.dot(s, w_ref[...].astype(BF16), preferred_element_type=F32) + b_ref[...]


def _adaln(c, ada_w, ada_b):
    r, d = c.shape
    n = ada_w.shape[1]
    tn = _tile(n, 1024)
    return pl.pallas_call(
        _ada_kernel,
        out_shape=jax.ShapeDtypeStruct((r, n), F32),
        grid=(n // tn,),
        in_specs=[
            pl.BlockSpec((r, d), lambda j: (0, 0)),
            pl.BlockSpec((d, tn), lambda j: (0, j)),
            pl.BlockSpec((1, tn), lambda j: (0, j)),
        ],
        out_specs=pl.BlockSpec((r, tn), lambda j: (0, j)),
        compiler_params=_params("parallel"),
        name="adaln",
    )(c, ada_w, ada_b.reshape(1, n))


def _prenorm_kernel(x_ref, g_ref, sc_ref, sh_ref, o_ref):
    x = x_ref[0]
    ms = jnp.mean(x * x, axis=-1, keepdims=True)
    y = x * lax.rsqrt(ms + EPS) * g_ref[...]
    o_ref[0] = (y * (1.0 + sc_ref[0]) + sh_ref[0]).astype(o_ref.dtype)


def _mod_spec(mod, tt, d):
    if mod.shape[1] == 1:
        return pl.BlockSpec((1, 1, d), lambda b, t: (b, 0, 0))
    return pl.BlockSpec((1, tt, d), lambda b, t: (b, t, 0))


def _prenorm(x, g, scale, shift):
    b, t, d = x.shape
    tt = _tile(t, 512)
    return pl.pallas_call(
        _prenorm_kernel,
        out_shape=jax.ShapeDtypeStruct((b, t, d), BF16),
        grid=(b, t // tt),
        in_specs=[
            pl.BlockSpec((1, tt, d), lambda b, t: (b, t, 0)),
            pl.BlockSpec((1, d), lambda b, t: (0, 0)),
            _mod_spec(scale, tt, d),
            _mod_spec(shift, tt, d),
        ],
        out_specs=pl.BlockSpec((1, tt, d), lambda b, t: (b, t, 0)),
        compiler_params=_params("parallel", "parallel"),
        name="prenorm",
    )(x, g.reshape(1, d), scale, shift)


def _proj_kernel(*refs, norm, out_scale, n_out):
    if norm:
        h_ref, w_ref, g_ref, gm_ref = refs[:4]
        outs = refs[4:4 + n_out]
    else:
        h_ref, w_ref = refs[:2]
        outs = refs[2:2 + n_out]
    wb_ref = refs[-1]

    @pl.when(pl.program_id(1) == 0)
    def _():
        wb_ref[...] = w_ref[...].astype(BF16)

    acc = jnp.dot(h_ref[...], wb_ref[...], preferred_element_type=F32)
    if not norm:
        for o in outs:
            o[...] = acc.astype(o.dtype)
        return
    tn = acc.shape[1]
    for cb in range(tn // LANES):
        y = acc[:, cb * LANES:(cb + 1) * LANES]
        ss = jnp.dot((y * y).astype(BF16), gm_ref[...], preferred_element_type=F32)
        yn = y * lax.rsqrt(ss * (1.0 / DA_QKDIM) + EPS) * g_ref[...]
        for o, s in zip(outs, out_scale):
            o[:, cb * LANES:(cb + 1) * LANES] = (yn * s if s != 1.0 else yn).astype(o.dtype)


def _proj(h, w, out_dtypes, *, col0=0, n=None, tn=None, norm_g=None, out_scale=None):
    m, k = h.shape
    n = w.shape[1] if n is None else n
    tm = _tile(m, 1024)
    tn = _tile(n, 1024 if tn is None else tn)
    assert col0 % tn == 0
    cb0 = col0 // tn
    norm = norm_g is not None
    n_out = len(out_dtypes)
    if out_scale is None:
        out_scale = (1.0,) * n_out
    in_specs = [
        pl.BlockSpec((tm, k), lambda j, i: (i, 0)),
        pl.BlockSpec((k, tn), lambda j, i: (0, cb0 + j)),
    ]
    args = [h, w]
    if norm:
        g2 = jnp.concatenate([norm_g, norm_g]).reshape(1, LANES).astype(F32)
        lane = jnp.arange(LANES) // DA_QKDIM
        gm = (lane[:, None] == lane[None, :]).astype(BF16)
        in_specs += [
            pl.BlockSpec((1, LANES), lambda j, i: (0, 0)),
            pl.BlockSpec((LANES, LANES), lambda j, i: (0, 0)),
        ]
        args += [g2, gm]
    outs = pl.pallas_call(
        functools.partial(_proj_kernel, norm=norm, out_scale=tuple(out_scale), n_out=n_out),
        out_shape=[jax.ShapeDtypeStruct((m, n), dt) for dt in out_dtypes],
        grid=(n // tn, m // tm),
        in_specs=in_specs,
        out_specs=[pl.BlockSpec((tm, tn), lambda j, i: (i, j)) for _ in out_dtypes],
        scratch_shapes=[pltpu.VMEM((k, tn), BF16)],
        compiler_params=_params("parallel", "arbitrary"),
        name="proj_norm" if norm else "proj",
    )(*args)
    return outs


def _lambda_value(lq1, lk1, lq2, lk2, lam_init):
    a = jnp.sum(lq1[...] * lk1[...], axis=-1, keepdims=True)
    b = jnp.sum(lq2[...] * lk2[...], axis=-1, keepdims=True)
    return jnp.exp(a) - jnp.exp(b) + lam_init


def _split_halves(q):
    lane = lax.broadcasted_iota(jnp.int32, q.shape, 1)
    zero = jnp.zeros_like(q)
    return jnp.concatenate(
        [jnp.where(lane < DA_QKDIM, q, zero), jnp.where(lane >= DA_QKDIM, q, zero)], axis=0)


def _attn_kernel(q_ref, k_ref, v_ref, lq1, lk1, lq2, lk2, g_ref, o_ref,
                 vx_ref, q2_ref, m_ref, acc_ref, sa_ref, sb_ref, *, bq, bk, sub, lam_init):
    qi = pl.program_id(2)
    rows = 2 * bq
    ratio = bq // bk

    @pl.when(qi == 0)
    def _():
        vx_ref[:, :DA_VDIM] = v_ref[0]
        vx_ref[:, DA_VDIM:] = jnp.ones((vx_ref.shape[0], DA_VDIM), vx_ref.dtype)

    q2_ref[...] = _split_halves(q_ref[0])
    m_ref[...] = jnp.full_like(m_ref, -jnp.inf)
    acc_ref[...] = jnp.zeros_like(acc_ref)

    def n_cols(q_lo, diag):
        return bk if diag is None else max(0, min(bk, q_lo + sub - diag * bk))

    def scores(ki, s_ref, diag=None):
        start = pl.multiple_of(ki * bk, bk)
        if diag is None:
            s_ref[...] = lax.dot_general(q2_ref[...], k_ref[0, pl.ds(start, bk), :],
                                         (((1,), (1,)), ((), ())), preferred_element_type=F32)
            return
        for r in range(rows // sub):
            rs = slice(r * sub, (r + 1) * sub)
            ncol = n_cols((r * sub) % bq, diag)
            if ncol:
                k = k_ref[0, pl.ds(start, ncol), :]
                s_ref[rs, :ncol] = lax.dot_general(q2_ref[rs], k, (((1,), (1,)), ((), ())),
                                                   preferred_element_type=F32)

    def update(ki, s_ref, diag=None):
        start = pl.multiple_of(ki * bk, bk)
        for r in range(rows // sub):
            rs = slice(r * sub, (r + 1) * sub)
            q_lo = (r * sub) % bq
            ncol = n_cols(q_lo, diag)
            if not ncol:
                continue
            vx = vx_ref[pl.ds(start, ncol), :]
            s = s_ref[rs, :ncol]
            if diag is not None and diag * bk + ncol - 1 > q_lo:
                row = q_lo + lax.broadcasted_iota(jnp.int32, (sub, ncol), 0)
                col = diag * bk + lax.broadcasted_iota(jnp.int32, (sub, ncol), 1)
                s = jnp.where(col <= row, s, -jnp.inf)
            m_prev = m_ref[rs]
            m_new = jnp.maximum(m_prev, jnp.max(s, axis=1, keepdims=True))
            alpha = jnp.exp2(m_prev - m_new)
            p = jnp.exp2(s - jnp.tile(m_new, (1, ncol // LANES))).astype(BF16)
            acc_ref[rs] = (acc_ref[rs] * jnp.tile(alpha, (1, 2))
                           + jnp.dot(p, vx, preferred_element_type=F32))
            m_ref[rs] = m_new

    scores(0, sa_ref)

    def pair(j, carry):
        scores(2 * j + 1, sb_ref)
        update(2 * j, sa_ref)
        scores(2 * j + 2, sa_ref)
        update(2 * j + 1, sb_ref)
        return carry

    lax.fori_loop(0, qi * (ratio // 2), pair, 0)

    kd = qi * ratio
    bufs = (sa_ref, sb_ref)
    for d in range(ratio):
        if d + 1 < ratio:
            scores(kd + d + 1, bufs[(d + 1) % 2], diag=d + 1)
        update(kd + d, bufs[d % 2], diag=d)

    acc = acc_ref[...]
    o = acc[:, :DA_VDIM] / acc[:, DA_VDIM:]
    lam = _lambda_value(lq1, lk1, lq2, lk2, lam_init)
    od = o[:bq] - lam * o[bq:]
    ms = jnp.mean(od * od, axis=-1, keepdims=True)
    o_ref[0] = (od * lax.rsqrt(ms + EPS) * g_ref[...] * (1.0 - lam_init)).astype(o_ref.dtype)


def _attn_prompt(q, k, v, lams, subln_g, lam_init):
    b, t, w = q.shape
    h = w // DA_VDIM
    bq = _tile(t, ATTN_Q_BLOCK)
    bk = _tile(bq, ATTN_K_BLOCK)
    assert (bq // bk) % 2 == 0, "the two-buffer score pipeline consumes key blocks in pairs"
    sub = _tile(bk, ATTN_ROW_SUB)
    vec = pl.BlockSpec((1, DA_QKDIM), lambda b, h, i: (0, 0))
    return pl.pallas_call(
        functools.partial(_attn_kernel, bq=bq, bk=bk, sub=sub, lam_init=lam_init),
        out_shape=jax.ShapeDtypeStruct((b, t, w), BF16),
        grid=(b, h, t // bq),
        in_specs=[
            pl.BlockSpec((1, bq, DA_VDIM), lambda b, h, i: (b, i, h)),
            pl.BlockSpec((1, t, DA_VDIM), lambda b, h, i: (b, 0, h)),
            pl.BlockSpec((1, t, DA_VDIM), lambda b, h, i: (b, 0, h)),
            vec, vec, vec, vec,
            pl.BlockSpec((1, DA_VDIM), lambda b, h, i: (0, 0)),
        ],
        out_specs=pl.BlockSpec((1, bq, DA_VDIM), lambda b, h, i: (b, i, h)),
        scratch_shapes=[
            pltpu.VMEM((t, 2 * DA_VDIM), BF16),
            pltpu.VMEM((2 * bq, DA_VDIM), BF16),
            pltpu.VMEM((2 * bq, LANES), F32),
            pltpu.VMEM((2 * bq, 2 * DA_VDIM), F32),
            pltpu.VMEM((2 * bq, bk), F32),
            pltpu.VMEM((2 * bq, bk), F32),
        ],
        compiler_params=_params("parallel", "parallel", "arbitrary"),
        name="diff_attn_prompt",
    )(q, k, v, *[x.reshape(1, DA_QKDIM) for x in lams], subln_g.reshape(1, DA_VDIM))


def _log_sigmoid(z):
    return jnp.minimum(z, 0.0) - jnp.log(1.0 + jnp.exp(-jnp.abs(z)))


def _row_to_col(row):
    n = row.shape[1]
    r = lax.broadcasted_iota(jnp.int32, (n, n), 0)
    c = lax.broadcasted_iota(jnp.int32, (n, n), 1)
    return jnp.sum(jnp.where(r == c, jnp.broadcast_to(row, (n, n)), 0.0), axis=1, keepdims=True)


def _split2(x):
    hi = x.astype(BF16)
    return hi, (x - hi.astype(F32)).astype(BF16)


def _dot_split(a, b):
    ah, al = _split2(a)
    bh, bl = _split2(b)
    return (jnp.dot(ah, bh, preferred_element_type=F32) + jnp.dot(ah, bl, preferred_element_type=F32)
            + jnp.dot(al, bh, preferred_element_type=F32))


def _gate_log(glr, w2, b):
    return _log_sigmoid(_dot_split(glr, w2) + b) * (1.0 / GATE_NORMALIZER)


def _gla_kernel(gq_ref, gk_ref, gv_ref, go_ref, glr_ref, w2_ref, b_ref,
                ng_ref, o_ref, sfin_ref, s_ref, *, tt, hg):
    ti = pl.program_id(2)

    @pl.when(ti == 0)
    def _():
        s_ref[...] = jnp.zeros_like(s_ref)

    c = GLA_CHUNK
    p2 = 2 * c
    kd, vd = GLA_KDIM, GLA_VDIM
    r = lax.broadcasted_iota(jnp.int32, (p2, p2), 0)
    cc = lax.broadcasted_iota(jnp.int32, (p2, p2), 1)
    same = (r >= c) == (cc >= c)
    tril2 = same & (r >= cc)
    lane_lo = cc < c
    tril_b = tril2.astype(BF16)
    rhs_t = jnp.concatenate([same & (r <= cc), r < c, r >= c], axis=1).astype(BF16)
    g_all, state = [], []
    for hh in range(hg):
        ks = slice(hh * kd, (hh + 1) * kd)
        g_all.append(_gate_log(glr_ref[0].astype(F32), w2_ref[:, ks], b_ref[:, ks]))
        state.append(s_ref[hh])
    for j in range(tt // p2):
        sl = slice(j * p2, (j + 1) * p2)
        for hh in range(hg):
            ks = slice(hh * kd, (hh + 1) * kd)
            vs = slice(hh * vd, (hh + 1) * vd)
            s_prev = state[hh]
            gh, gl = _split2(g_all[hh][sl])
            bc = (jnp.dot(tril_b, gh, preferred_element_type=F32)
                  + jnp.dot(tril_b, gl, preferred_element_type=F32))
            th, tl = _split2(g_all[hh][sl].T)
            pre = (jnp.dot(th, rhs_t, preferred_element_type=F32)
                   + jnp.dot(tl, rhs_t, preferred_element_type=F32))
            bc_t = pre[:, :p2]
            tot0 = pre[:, p2:2 * p2]
            tot1 = pre[:, 2 * p2:]
            q_t = (gq_ref[0, sl, ks].astype(F32) * (kd ** -0.5) * jnp.exp(bc)).astype(BF16)
            k_all = gk_ref[0, sl, ks].astype(F32).T
            k_t = (k_all * jnp.exp(-bc_t)).astype(BF16)
            k_e = k_all * jnp.exp(jnp.where(lane_lo, tot0, tot1) - bc_t)
            k_e0 = jnp.where(lane_lo, k_e, 0.0).astype(BF16)
            k_e1 = jnp.where(lane_lo, 0.0, k_e).astype(BF16)
            v = gv_ref[0, sl, vs]
            att = jnp.dot(q_t, k_t, preferred_element_type=F32)
            att = jnp.where(tril2, att, 0.0).astype(BF16)
            o_intra = jnp.dot(att, v, preferred_element_type=F32)
            s_mid = (jnp.tile(jnp.exp(tot0), (1, 2)) * s_prev
                     + jnp.dot(k_e0, v, preferred_element_type=F32))
            state[hh] = (jnp.tile(jnp.exp(tot1), (1, 2)) * s_mid
                         + jnp.dot(k_e1, v, preferred_element_type=F32))
            o = jnp.concatenate([
                o_intra[:c] + jnp.dot(q_t[:c], s_prev.astype(BF16), preferred_element_type=F32),
                o_intra[c:] + jnp.dot(q_t[c:], s_mid.astype(BF16), preferred_element_type=F32),
            ], axis=0)
            ms = jnp.mean(o * o, axis=-1, keepdims=True)
            gate = go_ref[0, sl, vs].astype(F32)
            gate = gate * jax.nn.sigmoid(gate)
            o_ref[0, sl, vs] = (o * lax.rsqrt(ms + EPS) * ng_ref[...] * gate).astype(o_ref.dtype)
    for hh in range(hg):
        s_ref[hh] = state[hh]

    @pl.when(ti == pl.num_programs(2) - 1)
    def _():
        for hh in range(hg):
            sfin_ref[0, hh] = state[hh]


def _gla_prompt(gall, gout, glr, w2p, gate_b, norm_g):
    b, t, _ = gall.shape
    tt = _tile(t, 512)
    assert tt % (2 * GLA_CHUNK) == 0
    hg = GLA_HEADS_PER_STEP
    kd, vd = hg * GLA_KDIM, hg * GLA_VDIM
    ng = GLA_HEADS // hg
    glr_blk = glr.shape[2] // LANES - 1
    return pl.pallas_call(
        functools.partial(_gla_kernel, tt=tt, hg=hg),
        out_shape=[
            jax.ShapeDtypeStruct((b, t, GLA_HEADS * GLA_VDIM), BF16),
            jax.ShapeDtypeStruct((b, GLA_HEADS, GLA_KDIM, GLA_VDIM), F32),
        ],
        grid=(b, ng, t // tt),
        in_specs=[
            pl.BlockSpec((1, tt, kd), lambda b, h, i: (b, i, h)),
            pl.BlockSpec((1, tt, kd), lambda b, h, i: (b, i, ng + h)),
            pl.BlockSpec((1, tt, vd), lambda b, h, i: (b, i, ng + h)),
            pl.BlockSpec((1, tt, vd), lambda b, h, i: (b, i, h)),
            pl.BlockSpec((1, tt, LANES), lambda b, h, i: (b, i, glr_blk)),
            pl.BlockSpec((LANES, kd), lambda b, h, i: (0, h)),
            pl.BlockSpec((1, kd), lambda b, h, i: (0, h)),
            pl.BlockSpec((1, GLA_VDIM), lambda b, h, i: (0, 0)),
        ],
        out_specs=[
            pl.BlockSpec((1, tt, vd), lambda b, h, i: (b, i, h)),
            pl.BlockSpec((1, hg, GLA_KDIM, GLA_VDIM), lambda b, h, i: (b, h, 0, 0)),
        ],
        scratch_shapes=[pltpu.VMEM((hg, GLA_KDIM, GLA_VDIM), F32)],
        compiler_params=_params("parallel", "parallel", "arbitrary"),
        name="gla_prompt",
    )(gall, gall, gall, gout, glr, w2p, gate_b.reshape(1, -1), norm_g.reshape(1, GLA_VDIM))


def _outproj_kernel(ma_ref, mb_ref, wa_ref, wb_ref, x_ref, g1_ref, ng_ref, sc_ref, sh_ref,
                    x1_ref, h2_ref):
    y = (jnp.dot(ma_ref[0], wa_ref[...], preferred_element_type=F32)
         + jnp.dot(mb_ref[0], wb_ref[...], preferred_element_type=F32))
    x1 = x_ref[0] + g1_ref[0] * y
    x1_ref[0] = x1
    ms = jnp.mean(x1 * x1, axis=-1, keepdims=True)
    h = x1 * lax.rsqrt(ms + EPS) * ng_ref[...]
    h2_ref[0] = (h * (1.0 + sc_ref[0]) + sh_ref[0]).astype(h2_ref.dtype)


def _outproj(mix_a, mix_b, w_a, w_b, x, gate1, norm2_g, scale2, shift2):
    b, t, d = x.shape
    ka = mix_a.shape[2]
    kb = mix_b.shape[2]
    tt = _tile(t, 512)
    row = lambda b, t: (b, t, 0)
    const = lambda b, t: (0, 0)
    return pl.pallas_call(
        _outproj_kernel,
        out_shape=[jax.ShapeDtypeStruct((b, t, d), F32), jax.ShapeDtypeStruct((b, t, d), BF16)],
        grid=(b, t // tt),
        in_specs=[
            pl.BlockSpec((1, tt, ka), row),
            pl.BlockSpec((1, tt, kb), row),
            pl.BlockSpec((ka, d), const),
            pl.BlockSpec((kb, d), const),
            pl.BlockSpec((1, tt, d), row),
            _mod_spec(gate1, tt, d),
            pl.BlockSpec((1, d), const),
            _mod_spec(scale2, tt, d),
            _mod_spec(shift2, tt, d),
        ],
        out_specs=[pl.BlockSpec((1, tt, d), row), pl.BlockSpec((1, tt, d), row)],
        compiler_params=_params("parallel", "parallel"),
        name="outproj",
    )(mix_a, mix_b, w_a, w_b, x, gate1, norm2_g.reshape(1, d), scale2, shift2)


def _mlp_kernel(h_ref, wu_ref, wd_ref, x_ref, g2_ref, o_ref, acc_ref):
    f = pl.program_id(2)

    @pl.when(f == 0)
    def _():
        acc_ref[...] = jnp.zeros_like(acc_ref)

    u = jnp.dot(h_ref[0], wu_ref[...], preferred_element_type=F32)
    u = jnp.maximum(u, 0.0)
    u = (u * u).astype(BF16)
    acc_ref[...] += jnp.dot(u, wd_ref[...], preferred_element_type=F32)

    @pl.when(f == pl.num_programs(2) - 1)
    def _():
        o_ref[0] = x_ref[0] + g2_ref[0] * acc_ref[...]


def _mlp(h2, w_up, w_down, x1, gate2):
    b, t, d = x1.shape
    ff = w_up.shape[1]
    tt = _tile(t, 512)
    tf = _tile(ff, 1024)
    if gate2.shape[1] == 1:
        g_spec = pl.BlockSpec((1, 1, d), lambda b, t, f: (b, 0, 0))
    else:
        g_spec = pl.BlockSpec((1, tt, d), lambda b, t, f: (b, t, 0))
    row = lambda b, t, f: (b, t, 0)
    return pl.pallas_call(
        _mlp_kernel,
        out_shape=jax.ShapeDtypeStruct((b, t, d), F32),
        grid=(b, t // tt, ff // tf),
        in_specs=[
            pl.BlockSpec((1, tt, d), row),
            pl.BlockSpec((d, tf), lambda b, t, f: (0, f)),
            pl.BlockSpec((tf, d), lambda b, t, f: (f, 0)),
            pl.BlockSpec((1, tt, d), row),
            g_spec,
        ],
        out_specs=pl.BlockSpec((1, tt, d), row),
        scratch_shapes=[pltpu.VMEM((tt, d), F32)],
        compiler_params=_params("parallel", "parallel", "arbitrary"),
        name="mlp",
    )(h2, w_up, w_down, x1, gate2)


def _decode_kernel(*refs, lam_init, n_grp):
    pt_ref, q_ref, kn_ref, vn_ref = refs[:4]
    ck_refs = refs[4:4 + n_grp]
    cv_refs = refs[4 + n_grp:4 + 2 * n_grp]
    lq1, lk1, lq2, lk2, g_ref, o_ref, m_ref, l_ref, acc_ref = refs[4 + 2 * n_grp:]
    del pt_ref
    p = pl.program_id(1)
    nh = DA_HEADS
    page = ck_refs[0].shape[0]
    cols = page * nh

    @pl.when(p == 0)
    def _():
        m_ref[...] = jnp.full_like(m_ref, -jnp.inf)
        l_ref[...] = jnp.zeros_like(l_ref)
        acc_ref[...] = jnp.zeros_like(acc_ref)

    q2 = _split_halves(q_ref[0]).astype(BF16)
    row_h = lax.broadcasted_iota(jnp.int32, (2 * nh, cols), 0) % nh
    col_h = lax.broadcasted_iota(jnp.int32, (2 * nh, cols), 1) % nh
    valid = row_h == col_h
    scores = []
    for ck_ref in ck_refs:
        kp = ck_ref[...].reshape(cols, DA_VDIM).astype(BF16)
        s = lax.dot_general(q2, kp, (((1,), (1,)), ((), ())), preferred_element_type=F32)
        scores.append(jnp.where(valid, s, -jnp.inf))
    m_prev = m_ref[...]
    m_new = m_prev
    for s in scores:
        m_new = jnp.maximum(m_new, jnp.max(s, axis=1, keepdims=True))
    alpha = jnp.exp2(m_prev - m_new)
    l = alpha * l_ref[...]
    acc = acc_ref[...] * alpha
    for s, cv_ref in zip(scores, cv_refs):
        pr = jnp.exp2(s - m_new)
        l = l + jnp.sum(pr, axis=1, keepdims=True)
        vp = cv_ref[...].reshape(cols, DA_VDIM).astype(BF16)
        acc = acc + jnp.dot(pr.astype(BF16), vp, preferred_element_type=F32)
    l_ref[...] = l
    acc_ref[...] = acc
    m_ref[...] = m_new

    @pl.when(p == pl.num_programs(1) - 1)
    def _():
        kn = kn_ref[0]
        vn = vn_ref[0]
        k2 = jnp.concatenate([kn, kn], axis=0)
        v2 = jnp.concatenate([vn, vn], axis=0)
        s_new = jnp.sum(q2.astype(F32) * k2, axis=1, keepdims=True)
        m_fin = jnp.maximum(m_new, s_new)
        a = jnp.exp2(m_new - m_fin)
        pn = jnp.exp2(s_new - m_fin)
        o = (acc * a + pn * v2) / (a * l + pn)
        lam = _lambda_value(lq1, lk1, lq2, lk2, lam_init)
        od = o[:nh] - lam * o[nh:]
        ms = jnp.mean(od * od, axis=-1, keepdims=True)
        o_ref[0] = od * lax.rsqrt(ms + EPS) * g_ref[...] * (1.0 - lam_init)


def _attn_sample(q, k_new, v_new, cache_k, cache_v, page_table, lams, subln_g, lam_init):
    db, nh, dv = q.shape
    n_pages = page_table.shape[1]
    page = cache_k.shape[1]
    n_grp = max(g for g in range(1, DECODE_PAGES_PER_STEP + 1) if n_pages % g == 0)
    tok = pl.BlockSpec((1, nh, dv), lambda b, p, pt: (b, 0, 0))
    cache = [pl.BlockSpec((None, page, nh, dv), lambda b, p, pt, j=j: (pt[b, p * n_grp + j], 0, 0, 0))
             for j in range(n_grp)]
    vec = pl.BlockSpec((1, DA_QKDIM), lambda b, p, pt: (0, 0))
    grid_spec = pltpu.PrefetchScalarGridSpec(
        num_scalar_prefetch=1,
        grid=(db, n_pages // n_grp),
        in_specs=[tok, tok, tok, *cache, *cache, vec, vec, vec, vec,
                  pl.BlockSpec((1, dv), lambda b, p, pt: (0, 0))],
        out_specs=pl.BlockSpec((1, nh, dv), lambda b, p, pt: (b, 0, 0)),
        scratch_shapes=[
            pltpu.VMEM((2 * nh, 1), F32),
            pltpu.VMEM((2 * nh, 1), F32),
            pltpu.VMEM((2 * nh, dv), F32),
        ],
    )
    return pl.pallas_call(
        functools.partial(_decode_kernel, lam_init=lam_init, n_grp=n_grp),
        out_shape=jax.ShapeDtypeStruct((db, nh, dv), F32),
        grid_spec=grid_spec,
        compiler_params=_params("parallel", "arbitrary"),
        name="diff_attn_sample",
    )(page_table, q, k_new, v_new, *([cache_k] * n_grp), *([cache_v] * n_grp),
      *[x.reshape(1, DA_QKDIM) for x in lams], subln_g.reshape(1, dv))


def _gla_step_kernel(g_ref, go_ref, glr_ref, w2_ref, b_ref, ng_ref, s0_ref, o_ref, s_ref):
    hk = GLA_HEADS * GLA_KDIM
    glr = jnp.broadcast_to(glr_ref[0].astype(F32), (8, LANES))
    g_all = _gate_log(glr, w2_ref[...], b_ref[...])[0:1]
    for h in range(GLA_HEADS):
        ks = slice(h * GLA_KDIM, (h + 1) * GLA_KDIM)
        q = g_ref[0, :, ks].astype(F32) * (GLA_KDIM ** -0.5)
        k = g_ref[0, :, hk + h * GLA_KDIM:hk + (h + 1) * GLA_KDIM].astype(F32)
        v = g_ref[0, :, 2 * hk + h * GLA_VDIM:2 * hk + (h + 1) * GLA_VDIM].astype(F32)
        gate = go_ref[0, :, h * GLA_VDIM:(h + 1) * GLA_VDIM].astype(F32)
        s = _row_to_col(jnp.exp(g_all[:, ks])) * s0_ref[h] + _row_to_col(k) * v
        s_ref[h] = s
        o = jnp.sum(_row_to_col(q) * s, axis=0, keepdims=True)
        ms = jnp.mean(o * o, axis=-1, keepdims=True)
        o_ref[0, :, h * GLA_VDIM:(h + 1) * GLA_VDIM] = (
            o * lax.rsqrt(ms + EPS) * ng_ref[...] * (gate * jax.nn.sigmoid(gate)))


def _gla_sample(gall, gout, glr, w2p, gate_b, norm_g, state):
    db = gall.shape[0]
    wg = gall.shape[2]
    const = lambda b: (0, 0)
    return pl.pallas_call(
        _gla_step_kernel,
        out_shape=[
            jax.ShapeDtypeStruct((db, 1, GLA_HEADS * GLA_VDIM), F32),
            jax.ShapeDtypeStruct(state.shape, F32),
        ],
        grid=(db,),
        in_specs=[
            pl.BlockSpec((1, 1, wg), lambda b: (b, 0, 0)),
            pl.BlockSpec((1, 1, gout.shape[2]), lambda b: (b, 0, 0)),
            pl.BlockSpec((1, 1, LANES), lambda b: (b, 0, glr.shape[2] // LANES - 1)),
            pl.BlockSpec((LANES, GLA_HEADS * GLA_KDIM), const),
            pl.BlockSpec((1, GLA_HEADS * GLA_KDIM), const),
            pl.BlockSpec((1, GLA_VDIM), const),
            pl.BlockSpec((None, GLA_HEADS, GLA_KDIM, GLA_VDIM), lambda b: (b, 0, 0, 0)),
        ],
        out_specs=[
            pl.BlockSpec((1, 1, GLA_HEADS * GLA_VDIM), lambda b: (b, 0, 0)),
            pl.BlockSpec((None, GLA_HEADS, GLA_KDIM, GLA_VDIM), lambda b: (b, 0, 0, 0)),
        ],
        compiler_params=_params("parallel"),
        name="gla_sample",
    )(gall, gout, glr, w2p, gate_b.reshape(1, -1), norm_g.reshape(1, GLA_VDIM), state)


def _layer_weights(w_in, gate_w2, w_out, w_up, w_down):
    da = DA_HEADS * DA_VDIM
    gv = GLA_HEADS * GLA_VDIM
    c_glr = 3 * da + 2 * GLA_HEADS * GLA_KDIM + gv
    c_gout = c_glr + GATE_RANK
    w_gog = jnp.concatenate(
        [w_in[:, c_gout:c_gout + gv], jnp.pad(w_in[:, c_glr:c_gout], ((0, 0), (0, LANES - GATE_RANK)))],
        axis=1)
    w2p = jnp.pad(gate_w2.astype(F32), ((0, LANES - GATE_RANK), (0, 0)))
    wo = w_out.astype(BF16)
    return dict(
        w_in=w_in, w_gog=w_gog, w2p=w2p,
        wo_a=wo[:da], wo_b=wo[da:], w_up=w_up.astype(BF16), w_down=w_down.astype(BF16))


def _mixer_inputs(x, scale1, shift1, norm1_g, lw, q_norm_g, k_norm_g):
    b, t, d = x.shape
    da = DA_HEADS * DA_VDIM
    n_gqkv = 2 * GLA_HEADS * GLA_KDIM + GLA_HEADS * GLA_VDIM
    h = _prenorm(x, norm1_g, scale1, shift1).reshape(b * t, d)
    w_in = lw["w_in"]
    (q,) = _proj(h, w_in, [BF16], col0=0, n=da, norm_g=q_norm_g, out_scale=(Q_SCALE,))
    k32, k16 = _proj(h, w_in, [F32, BF16], col0=da, n=da, norm_g=k_norm_g)
    v32, v16 = _proj(h, w_in, [F32, BF16], col0=2 * da, n=da)
    (gqkv,) = _proj(h, w_in, [BF16], col0=3 * da, n=n_gqkv)
    (gog,) = _proj(h, lw["w_gog"], [BF16], tn=lw["w_gog"].shape[1])
    return q, k32, k16, v32, v16, gqkv, gog, gog


def kernel(x_prompt, x_sample, cache_k, cache_v, state_gla, page_table, c_prompt, c_sample, norm1_g, ada_w, ada_b, w_in, q_norm_g, k_norm_g, lambda_q1, lambda_k1, lambda_q2, lambda_k2, diff_subln_g, gla_gate_w2, gla_gate_b, gla_norm_g, w_out, norm2_g, w_up, w_down):
    depth = w_in.shape[0]
    bp, tp, d = x_prompt.shape
    db, ts, _ = x_sample.shape
    assert ts == 1, "sample group decodes one token per sequence"
    da = DA_HEADS * DA_VDIM
    xp = x_prompt
    xs = x_sample.reshape(1, db, d)
    outs = [[] for _ in range(6)]
    n_ctl = bp + db
    n_pad = -n_ctl % 16
    c_all = jnp.pad(jnp.concatenate([c_prompt, c_sample], axis=0), ((0, n_pad), (0, 0)))
    for i in range(depth):
        lam_init = 0.8 - 0.6 * math.exp(-0.3 * i)
        lams = (lambda_q1[i], lambda_k1[i], lambda_q2[i], lambda_k2[i])
        lw = _layer_weights(w_in[i], gla_gate_w2[i], w_out[i], w_up[i], w_down[i])
        mod = _adaln(c_all, ada_w[i], ada_b[i])
        mods_p = [m[:bp, None, :] for m in jnp.split(mod, 6, axis=-1)]
        mods_s = [m[None, bp:bp + db, :] for m in jnp.split(mod, 6, axis=-1)]

        sh1, sc1, g1, sh2, sc2, g2 = mods_p
        q, k32, k16, v32, v16, gall, gout, glr = _mixer_inputs(
            xp, sc1, sh1, norm1_g[i], lw, q_norm_g[i], k_norm_g[i])
        o_da = _attn_prompt(q.reshape(bp, tp, da), k16.reshape(bp, tp, da), v16.reshape(bp, tp, da),
                            lams, diff_subln_g[i], lam_init)
        o_gla, s_p = _gla_prompt(gall.reshape(bp, tp, -1), gout.reshape(bp, tp, -1),
                                 glr.reshape(bp, tp, -1), lw["w2p"], gla_gate_b[i], gla_norm_g[i])
        outs[0].append(k32.reshape(bp, tp, DA_HEADS, DA_VDIM))
        outs[1].append(v32.reshape(bp, tp, DA_HEADS, DA_VDIM))
        outs[2].append(s_p)
        x1, h2 = _outproj(o_da, o_gla, lw["wo_a"], lw["wo_b"], xp, g1, norm2_g[i], sc2, sh2)
        xp = _mlp(h2, lw["w_up"], lw["w_down"], x1, g2)

        sh1, sc1, g1, sh2, sc2, g2 = mods_s
        q, k32, k16, v32, v16, gall, gout, glr = _mixer_inputs(
            xs, sc1, sh1, norm1_g[i], lw, q_norm_g[i], k_norm_g[i])
        o_da = _attn_sample(q.astype(F32).reshape(db, DA_HEADS, DA_VDIM), k32.reshape(db, DA_HEADS, DA_VDIM),
                            v32.reshape(db, DA_HEADS, DA_VDIM), cache_k[i], cache_v[i], page_table,
                            lams, diff_subln_g[i], lam_init)
        o_gla, s_s = _gla_sample(gall.reshape(db, 1, -1), gout.reshape(db, 1, -1),
                                 glr.reshape(db, 1, -1), lw["w2p"],
                                 gla_gate_b[i], gla_norm_g[i], state_gla[i])
        outs[3].append(k32.reshape(db, 1, DA_HEADS, DA_VDIM))
        outs[4].append(v32.reshape(db, 1, DA_HEADS, DA_VDIM))
        outs[5].append(s_s)
        x1, h2 = _outproj(o_da.reshape(1, db, da).astype(BF16), o_gla.reshape(1, db, -1).astype(BF16),
                          lw["wo_a"], lw["wo_b"], xs, g1, norm2_g[i], sc2, sh2)
        xs = _mlp(h2, lw["w_up"], lw["w_down"], x1, g2)

    return (xp, xs.reshape(db, 1, d), *[jnp.stack(o) for o in outs])
```
